```python
import jax, jax.numpy as jnp
from jax import lax
import numpy as np

D_MODEL = 1024
BATCH = 8
SEQ = 2048
DEPTH = 2

GRID_W = 64
CTX_LEN = 256
N_MIXERS = 2
SSD_EXPAND = 2
SSD_D_INNER = SSD_EXPAND * D_MODEL
SSD_HEAD_DIM = 64
SSD_HEADS = SSD_D_INNER // SSD_HEAD_DIM
SSD_GROUPS = 8
SSD_HPG = SSD_HEADS // SSD_GROUPS
SSD_STATE = 128
SSD_CONV = 5
SSD_CHUNK = 128
SSD_CONV_DIM = SSD_D_INNER + 2 * SSD_GROUPS * SSD_STATE
SSD_IN_DIM = SSD_D_INNER + SSD_CONV_DIM + 2 * SSD_HEADS
CONF_KERNEL = 31
FFN_HIDDEN = ((8 * D_MODEL // 3 + 255) // 256) * 256
FFN_CONV = 3
N_SSD_LAYERS = (DEPTH + 1) // 2
N_CONF_LAYERS = DEPTH // 2
EPS = 1e-6

kernel_name = 'hybrid_ssd_conformer_dit_ctx_prefix'


def rmsnorm(h, w):
    hf = h.astype(jnp.float32)
    y = hf * lax.rsqrt(jnp.mean(hf * hf, axis=-1, keepdims=True) + EPS)
    return (y * w.astype(jnp.float32)).astype(h.dtype)


def layernorm(h, w, b):
    hf = h.astype(jnp.float32)
    mu = jnp.mean(hf, axis=-1, keepdims=True)
    d = hf - mu
    y = d * lax.rsqrt(jnp.mean(d * d, axis=-1, keepdims=True) + EPS)
    return (y * w.astype(jnp.float32) + b.astype(jnp.float32)).astype(h.dtype)


def modulate(h, g, shift, scale):
    return rmsnorm(h, g) * (1 + scale) + shift


def ada_params(cond, w, b):
    m = jax.nn.silu(cond) @ w + b
    return jnp.split(m, 6, axis=-1)


def dwconv1d(u, w, b):
    k, ch = w.shape
    pad = k // 2
    y = lax.conv_general_dilated(u, w[:, None, :].astype(u.dtype), window_strides=(1,),
                                 padding=[(pad, pad)], dimension_numbers=('NWC', 'WIO', 'NWC'),
                                 feature_group_count=ch)
    return y + b


def dwconv2d_grid(u, w, b):
    bsz, l, ch = u.shape
    rows = l // GRID_W
    u4 = u.reshape(bsz, rows, GRID_W, ch)
    kh, kw, _ = w.shape
    y = lax.conv_general_dilated(u4, w[:, :, None, :].astype(u.dtype), window_strides=(1, 1),
                                 padding=[(kh // 2, kh // 2), (kw // 2, kw // 2)],
                                 dimension_numbers=('NHWC', 'HWIO', 'NHWC'),
                                 feature_group_count=ch)
    return y.reshape(bsz, l, ch) + b


def ssd_scan(x, dt, A, B, C, s0):
    bsz, l, g, r, p = x.shape
    n = B.shape[-1]
    q = SSD_CHUNK
    nc = l // q
    x = x.astype(jnp.float32).reshape(bsz, nc, q, g, r, p)
    dt = dt.reshape(bsz, nc, q, g, r)
    B = B.astype(jnp.float32).reshape(bsz, nc, q, g, n)
    C = C.astype(jnp.float32).reshape(bsz, nc, q, g, n)
    acum = jnp.cumsum(dt * A, axis=2)
    xdt = x * dt[..., None]
    seg = acum[:, :, :, None] - acum[:, :, None, :]
    mask = jnp.tril(jnp.ones((q, q), dtype=bool))[:, :, None, None]
    decay = jnp.exp(jnp.where(mask, seg, -jnp.inf))
    cb = jnp.einsum('bcign,bcjgn->bcijg', C, B)
    y_diag = jnp.einsum('bcijgr,bcjgrp->bcigrp', cb[..., None] * decay, xdt)
    decay_to_end = jnp.exp(acum[:, :, -1:] - acum)
    chunk_states = jnp.einsum('bcjgn,bcjgrp->bcgrpn', B, xdt * decay_to_end[..., None])
    chunk_decay = jnp.exp(acum[:, :, -1])

    def step(s, inp):
        dec, st = inp
        return dec[..., None, None] * s + st, s

    final, entering = lax.scan(step, s0.astype(jnp.float32),
                               (jnp.moveaxis(chunk_decay, 1, 0), jnp.moveaxis(chunk_states, 1, 0)))
    entering = jnp.moveaxis(entering, 0, 1)
    y_off = jnp.einsum('bcign,bcgrpn->bcigrp', C, entering) * jnp.exp(acum)[..., None]
    y = (y_diag + y_off).reshape(bsz, l, g, r, p)
    return y, final


def ssd_mixer(u, w_in, conv_w, conv_b, dt_bias, a_log, d_skip, norm_w, w_out, s0_fwd, s0_bwd):
    bsz, l, _ = u.shape
    di, gn = SSD_D_INNER, SSD_GROUPS * SSD_STATE
    proj = u @ w_in
    z = proj[..., :di]
    xbc = jax.nn.silu(dwconv1d(proj[..., di:di + SSD_CONV_DIM], conv_w, conv_b))
    dt_raw = proj[..., di + SSD_CONV_DIM:]
    xs = xbc[..., :di].reshape(bsz, l, SSD_GROUPS, SSD_HPG, SSD_HEAD_DIM)
    Bm = xbc[..., di:di + gn].reshape(bsz, l, SSD_GROUPS, SSD_STATE)
    Cm = xbc[..., di + gn:].reshape(bsz, l, SSD_GROUPS, SSD_STATE)
    dt = jax.nn.softplus(dt_raw.astype(jnp.float32).reshape(bsz, l, 2, SSD_GROUPS, SSD_HPG)
                         + dt_bias.astype(jnp.float32).reshape(2, SSD_GROUPS, SSD_HPG))
    A = -jnp.exp(a_log.astype(jnp.float32)).reshape(2, SSD_GROUPS, SSD_HPG)
    y_f, s_f = ssd_scan(xs, dt[:, :, 0], A[0], Bm, Cm, s0_fwd)
    y_b, s_b = ssd_scan(jnp.flip(xs, 1), jnp.flip(dt[:, :, 1], 1), A[1],
                        jnp.flip(Bm, 1), jnp.flip(Cm, 1), s0_bwd)
    y = y_f + jnp.flip(y_b, 1) + d_skip.astype(jnp.float32).reshape(SSD_GROUPS, SSD_HPG)[:, :, None] * xs.astype(jnp.float32)
    y = y.reshape(bsz, l, di) * jax.nn.silu(z.astype(jnp.float32))
    out = rmsnorm(y, norm_w).astype(u.dtype) @ w_out
    return out, (s_f, s_b)


def conformer_conv(u, w1, b1, w_dw, b_dw, ln_w, ln_b, w2, b2):
    h = u @ w1 + b1
    a, g = jnp.split(h, 2, axis=-1)
    h = a * jax.nn.sigmoid(g)
    h = dwconv1d(h, w_dw, b_dw)
    h = jax.nn.silu(layernorm(h, ln_w, ln_b))
    return h @ w2 + b2


def conv_ffn(u, w_up, conv_w, conv_b, w_down, on_grid):
    h = u @ w_up
    val, gate = jnp.split(h, 2, axis=-1)
    if on_grid:
        gate = dwconv2d_grid(gate, conv_w, conv_b)
    else:
        gate = dwconv1d(gate, conv_w[FFN_CONV // 2], conv_b)
    return (jax.nn.silu(gate) * val) @ w_down


def setup_inputs(seed: int = 0) -> dict:
    key = jax.random.key(seed)
    ks = jax.random.split(key, 32)
    f32 = jnp.float32

    def nrm(k, shape, scale):
        return jax.random.normal(k, shape, f32) * scale

    D = D_MODEL
    u = jax.random.uniform(ks[10], (N_SSD_LAYERS, 2, SSD_HEADS), f32)
    dt0 = jnp.exp(u * (np.log(0.1) - np.log(0.001)) + np.log(0.001)).astype(f32)
    dt_bias = dt0 + jnp.log(-jnp.expm1(-dt0))
    a_log = jnp.log(jax.random.uniform(ks[11], (N_SSD_LAYERS, 2, SSD_HEADS), f32, 1.0, 16.0))
    return {
        'x': nrm(ks[0], (BATCH, SEQ, D), 1.0),
        'c': nrm(ks[1], (BATCH, D), 1.0),
        'ctx': nrm(ks[2], (BATCH, CTX_LEN, D), 1.0),
        'c_ctx': nrm(ks[3], (D,), 1.0),
        'mod_w': nrm(ks[4], (DEPTH, D, 6 * D), 0.5 * D ** -0.5),
        'mod_b': nrm(ks[5], (DEPTH, 6 * D), 0.02),
        'norm1_w': 1.0 + nrm(ks[6], (DEPTH, D), 0.02),
        'norm2_w': 1.0 + nrm(ks[7], (DEPTH, D), 0.02),
        'ssd_w_in': nrm(ks[8], (N_SSD_LAYERS, D, SSD_IN_DIM), D ** -0.5),
        'ssd_conv_w': nrm(ks[9], (N_SSD_LAYERS, SSD_CONV, SSD_CONV_DIM), SSD_CONV ** -0.5),
        'ssd_conv_b': nrm(ks[12], (N_SSD_LAYERS, SSD_CONV_DIM), 0.02),
        'ssd_dt_bias': dt_bias,
        'ssd_a_log': a_log,
        'ssd_d': 1.0 + nrm(ks[13], (N_SSD_LAYERS, SSD_HEADS), 0.1),
        'ssd_norm_w': 1.0 + nrm(ks[14], (N_SSD_LAYERS, SSD_D_INNER), 0.02),
        'ssd_w_out': nrm(ks[15], (N_SSD_LAYERS, SSD_D_INNER, D), SSD_D_INNER ** -0.5),
        'conf_w_pw1': nrm(ks[16], (N_CONF_LAYERS, D, 2 * D), D ** -0.5),
        'conf_b_pw1': nrm(ks[17], (N_CONF_LAYERS, 2 * D), 0.02),
        'conf_w_dw': nrm(ks[18], (N_CONF_LAYERS, CONF_KERNEL, D), CONF_KERNEL ** -0.5),
        'conf_b_dw': nrm(ks[19], (N_CONF_LAYERS, D), 0.02),
        'conf_ln_w': 1.0 + nrm(ks[20], (N_CONF_LAYERS, D), 0.02),
        'conf_ln_b': nrm(ks[21], (N_CONF_LAYERS, D), 0.02),
        'conf_w_pw2': nrm(ks[22], (N_CONF_LAYERS, D, D), D ** -0.5),
        'conf_b_pw2': nrm(ks[23], (N_CONF_LAYERS, D), 0.02),
        'ffn_w_up': nrm(ks[24], (DEPTH, D, 2 * FFN_HIDDEN), D ** -0.5),
        'ffn_conv_w': nrm(ks[25], (DEPTH, FFN_CONV, FFN_CONV, FFN_HIDDEN), (FFN_CONV * FFN_CONV) ** -0.5),
        'ffn_conv_b': nrm(ks[26], (DEPTH, FFN_HIDDEN), 0.02),
        'ffn_w_down': nrm(ks[27], (DEPTH, FFN_HIDDEN, D), FFN_HIDDEN ** -0.5),
        'final_norm_w': 1.0 + nrm(ks[28], (D,), 0.02),
    }


def reference(x, c, ctx, c_ctx, mod_w, mod_b, norm1_w, norm2_w,
              ssd_w_in, ssd_conv_w, ssd_conv_b, ssd_dt_bias, ssd_a_log, ssd_d, ssd_norm_w, ssd_w_out,
              conf_w_pw1, conf_b_pw1, conf_w_dw, conf_b_dw, conf_ln_w, conf_ln_b, conf_w_pw2, conf_b_pw2,
              ffn_w_up, ffn_conv_w, ffn_conv_b, ffn_w_down, final_norm_w):
    h, hc = x, ctx
    bsz = x.shape[0]
    for i in range(DEPTH):
        kind = i % N_MIXERS
        j = i // N_MIXERS
        last = i == DEPTH - 1
        need_ctx = (not last) or kind == 0
        sh1, sc1, g1, sh2, sc2, g2 = ada_params(c[:, None, :], mod_w[i], mod_b[i])
        a = modulate(h, norm1_w[i], sh1, sc1)
        if need_ctx:
            csh1, csc1, cg1, csh2, csc2, cg2 = ada_params(c_ctx, mod_w[i], mod_b[i])
            ac = modulate(hc, norm1_w[i], csh1, csc1)
        if kind == 0:
            ssd_p = (ssd_w_in[j], ssd_conv_w[j], ssd_conv_b[j], ssd_dt_bias[j], ssd_a_log[j],
                     ssd_d[j], ssd_norm_w[j], ssd_w_out[j])
            zeros = jnp.zeros((bsz, SSD_GROUPS, SSD_HPG, SSD_HEAD_DIM, SSD_STATE), jnp.float32)
            yc, (s_f, s_b) = ssd_mixer(ac, *ssd_p, zeros, zeros)
            y, _ = ssd_mixer(a, *ssd_p, s_f, s_b)
        else:
            conf_p = (conf_w_pw1[j], conf_b_pw1[j], conf_w_dw[j], conf_b_dw[j],
                      conf_ln_w[j], conf_ln_b[j], conf_w_pw2[j], conf_b_pw2[j])
            y = conformer_conv(a, *conf_p)
            if not last:
                yc = conformer_conv(ac, *conf_p)
        h = h + g1 * y
        h = h + g2 * conv_ffn(modulate(h, norm2_w[i], sh2, sc2), ffn_w_up[i], ffn_conv_w[i],
                              ffn_conv_b[i], ffn_w_down[i], True)
        if not last:
            hc = hc + cg1 * yc
            hc = hc + cg2 * conv_ffn(modulate(hc, norm2_w[i], csh2, csc2), ffn_w_up[i], ffn_conv_w[i],
                                     ffn_conv_b[i], ffn_w_down[i], False)
    return rmsnorm(h, final_norm_w)
```

```python
import functools

import jax
import jax.numpy as jnp
from jax import lax
from jax.experimental import pallas as pl
from jax.experimental.pallas import tpu as pltpu

F32 = jnp.float32
BF16 = jnp.bfloat16

EPS = 1e-6
GRID_W = 64
SSD_GROUPS = 8
SSD_HPG = 4
SSD_HEAD_DIM = 64
SSD_STATE = 128
SSD_CHUNK = 128
SSD_CONV = 5
CONF_KERNEL = 31
FFN_CONV = 3
FFN_GUARD = 8

VMEM_LIMIT_BYTES = 56 * 1024 * 1024


def _params(*semantics):
    return pltpu.CompilerParams(dimension_semantics=semantics,
                                vmem_limit_bytes=VMEM_LIMIT_BYTES)


def _sigmoid(v):
    return 1.0 / (1.0 + jnp.exp(-v))


def _silu(v):
    return v * _sigmoid(v)


def _resident(shape):
    nd = len(shape)
    return pl.BlockSpec(shape, lambda *_: (0,) * nd, pipeline_mode=pl.Buffered(1))


def _ada_kernel(cond_ref, w_ref, b_ref, o_ref):
    a = _silu(cond_ref[...]).astype(BF16)
    o_ref[0] = jnp.dot(a, w_ref[0].astype(BF16), preferred_element_type=F32) + b_ref[0]


def _ada_params(cond, mod_w, mod_b, tn=1536):
    depth, d, n = mod_w.shape
    rows = cond.shape[0]
    return pl.pallas_call(
        _ada_kernel,
        grid=(depth, n // tn),
        in_specs=[pl.BlockSpec((rows, d), lambda i, j: (0, 0)),
                  pl.BlockSpec((1, d, tn), lambda i, j: (i, 0, j)),
                  pl.BlockSpec((1, 1, tn), lambda i, j: (i, 0, j))],
        out_specs=pl.BlockSpec((1, rows, tn), lambda i, j: (i, 0, j)),
        out_shape=jax.ShapeDtypeStruct((depth, rows, n), F32),
        compiler_params=_params("parallel", "parallel"),
        name="ada_params",
    )(cond, mod_w, mod_b.reshape(depth, 1, n))


def _mm_kernel(*refs, modulate, post, has_bias, n_out, n_chunk):
    it = iter(refs)
    x_ref = next(it)
    g_ref = next(it)
    sh_ref = sc_ref = b_ref = res_ref = gate_ref = None
    if modulate:
        sh_ref, sc_ref = next(it), next(it)
    w_ref = next(it)
    if has_bias:
        b_ref = next(it)
    if post == "resid":
        res_ref, gate_ref = next(it), next(it)
    o_ref = next(it)

    x = x_ref[0]
    ms = jnp.mean(x * x, axis=-1, keepdims=True)
    y = (x * lax.rsqrt(ms + EPS)) * g_ref[...]
    if modulate:
        y = y * (1.0 + sc_ref[0]) + sh_ref[0]
    yb = y.astype(BF16)

    def proj(c0):
        acc = jnp.dot(yb, w_ref[:, c0:c0 + n_chunk], preferred_element_type=F32)
        if has_bias:
            acc = acc + b_ref[:, c0:c0 + n_chunk]
        return acc

    for n0 in range(0, n_out, n_chunk):
        if post == "glu":
            out = proj(n0) * _sigmoid(proj(n_out + n0))
        elif post == "resid":
            out = res_ref[0, :, n0:n0 + n_chunk] + gate_ref[0, :, n0:n0 + n_chunk] * proj(n0)
        else:
            out = proj(n0)
        o_ref[0, :, n0:n0 + n_chunk] = out.astype(o_ref.dtype)


def _norm_matmul(x, g, w, *, shift=None, scale=None, bias=None, post="store",
                 res=None, gate=None, out_dtype=F32, tm=512, n_chunk=512, name=None):
    bsz, t, k = x.shape
    nw = w.shape[1]
    n_out = nw // 2 if post == "glu" else nw
    tm = min(tm, t)
    n_chunk = max(c for c in range(128, n_chunk + 1, 128) if n_out % c == 0)
    assert t % tm == 0
    modulate = shift is not None
    row = lambda b, i: (b, i, 0)
    per_batch = lambda b, i: (b, 0, 0)
    args = [x, g.reshape(1, k)]
    specs = [pl.BlockSpec((1, tm, k), row), _resident((1, k))]
    if modulate:
        args += [shift, scale]
        specs += [pl.BlockSpec((1, 1, k), per_batch)] * 2
    args.append(w)
    specs.append(_resident((k, nw)))
    if bias is not None:
        args.append(bias.reshape(1, nw))
        specs.append(_resident((1, nw)))
    if post == "resid":
        args += [res, gate]
        specs += [pl.BlockSpec((1, tm, n_out), row), pl.BlockSpec((1, 1, n_out), per_batch)]
    kern = functools.partial(_mm_kernel, modulate=modulate, post=post,
                             has_bias=bias is not None, n_out=n_out, n_chunk=n_chunk)
    return pl.pallas_call(
        kern,
        grid=(bsz, t // tm),
        in_specs=specs,
        out_specs=pl.BlockSpec((1, tm, n_out), row),
        out_shape=jax.ShapeDtypeStruct((bsz, t, n_out), out_dtype),
        compiler_params=_params("parallel", "parallel"),
        name=name,
    )(*args)


def _split3(v):
    v1 = v.astype(BF16)
    r = v - v1.astype(F32)
    v2 = r.astype(BF16)
    r = r - v2.astype(F32)
    return v1, v2, r.astype(BF16)


def _softplus(v):
    return jnp.maximum(v, 0.0) + jnp.log1p(jnp.exp(-jnp.abs(v)))


def _ssd_kernel(xp_ref, bp_ref, cp_ref, z_ref, dtc_ref, dtr_ref,
                wx_ref, wb_ref, wc_ref, bx_ref, bb_ref, bc_ref,
                pc_ref, pr_ref, d_ref, s0_ref,
                y_ref, sn_ref,
                xs_ref, bs_ref, cs_ref, st_ref, pad_ref, *, seq):
    q = SSD_CHUNK
    nc = seq // q
    hd = SSD_HEAD_DIM
    halo = 8

    def conv_silu(src_ref, w_ref, b_ref, dst_ref):
        ch = dst_ref.shape[1]
        pad_ref[0:halo, 0:ch] = jnp.zeros((halo, ch), F32)
        pad_ref[halo + seq:2 * halo + seq, 0:ch] = jnp.zeros((halo, ch), F32)
        pad_ref[halo:halo + seq, 0:ch] = src_ref[0]
        w = w_ref[...]
        bias = b_ref[...]

        def body(rb, carry):
            r0 = pl.multiple_of(rb * q, q)
            win = pad_ref[pl.ds(r0, q + 2 * halo), 0:ch]
            acc = bias
            for k in range(SSD_CONV):
                off = halo - SSD_CONV // 2 + k
                acc = acc + w[k:k + 1, :] * win[off:off + q, :]
            dst_ref[pl.ds(r0, q), :] = _silu(acc)
            return carry

        lax.fori_loop(0, nc, body, 0)

    conv_silu(xp_ref, wx_ref, bx_ref, xs_ref)
    conv_silu(bp_ref, wb_ref, bb_ref, bs_ref)
    conv_silu(cp_ref, wc_ref, bc_ref, cs_ref)

    st_ref[...] = s0_ref[0, 0]
    y_ref[0] = d_ref[...] * xs_ref[...]

    ii = lax.broadcasted_iota(jnp.int32, (q, q), 0)
    jj = lax.broadcasted_iota(jnp.int32, (q, q), 1)
    lower = jj <= ii
    upper = jj >= ii
    lower_b = jnp.where(lower, 1.0, 0.0).astype(BF16)
    upper_b = jnp.where(upper, 1.0, 0.0).astype(BF16)

    def direction(d, r0, b_c, c_c, bt_c, cb, x_c):
        tri = lower if d == 0 else upper
        col_m = lower_b if d == 0 else upper_b
        row_m = upper_b if d == 0 else lower_b
        bias_c = pc_ref[0, 0, d:d + 1, :]
        a_c = -jnp.exp(pc_ref[0, 1, d:d + 1, :])
        bias_r = pr_ref[0, 0, d]
        a_r = -jnp.exp(pr_ref[0, 1, d])
        dt_c = _softplus(dtc_ref[0, 0, d, pl.ds(r0, q), :] + bias_c)
        dt_r = _softplus(dtr_ref[0, 0, d, :, pl.ds(r0, q)] + bias_r)
        ad_c = dt_c * a_c
        ad_r = dt_r * a_r
        cum_c = sum(jnp.dot(col_m, p, preferred_element_type=F32) for p in _split3(ad_c))
        cum_r = sum(jnp.dot(p, row_m, preferred_element_type=F32) for p in _split3(ad_r))
        tot_r = jnp.sum(ad_r, axis=1, keepdims=True)
        ys = []
        for h in range(SSD_HPG):
            ac = cum_c[:, h:h + 1]
            ar = cum_r[h:h + 1, :]
            dtr = dt_r[h:h + 1, :]
            tot = tot_r[h:h + 1, :]
            lmat = jnp.exp(jnp.where(tri, ac - ar, -jnp.inf)) * cb * dtr
            c_h = c_c * jnp.exp(ac)
            x_h = x_c[:, h * hd:(h + 1) * hd]
            s_h = st_ref[d, h]
            y_h = (jnp.dot(lmat.astype(BF16), x_h, preferred_element_type=F32)
                   + jnp.dot(c_h.astype(BF16), s_h.astype(BF16), preferred_element_type=F32))
            ys.append(y_h)
            bw = bt_c * (dtr * jnp.exp(tot - ar))
            st_ref[d, h] = jnp.exp(tot) * s_h + jnp.dot(bw.astype(BF16), x_h,
                                                         preferred_element_type=F32)
        rows = pl.ds(r0, q)
        y_ref[0, rows, :] = y_ref[0, rows, :] + jnp.concatenate(ys, axis=1)

    def load_chunk(r0):
        rows = pl.ds(r0, q)
        b_c = bs_ref[rows, :]
        c_c = cs_ref[rows, :]
        cb = lax.dot_general(c_c.astype(BF16), b_c.astype(BF16), (((1,), (1,)), ((), ())),
                             preferred_element_type=F32)
        return b_c, c_c, b_c.T, cb, xs_ref[rows, :].astype(BF16)

    def body(c, carry):
        rf = pl.multiple_of(c * q, q)
        rb = pl.multiple_of((nc - 1 - c) * q, q)
        direction(0, rf, *load_chunk(rf))
        direction(1, rb, *load_chunk(rb))
        return carry

    lax.fori_loop(0, nc, body, 0)

    sn_ref[0, 0] = st_ref[...]

    def gate_body(rb, carry):
        rows = pl.ds(pl.multiple_of(rb * q, q), q)
        y_ref[0, rows, :] = y_ref[0, rows, :] * _silu(z_ref[0, rows, :])
        return carry

    lax.fori_loop(0, nc, gate_body, 0)


def _ssd_scan(proj, dt_cols, dt_rows, conv_w, conv_b, pc, pr, d_lanes, s0):
    bsz, seq, _ = proj.shape
    g, h, hd, n = SSD_GROUPS, SSD_HPG, SSD_HEAD_DIM, SSD_STATE
    xw = h * hd
    di = g * xw
    x_blk0 = di // xw
    b_blk0 = (2 * di) // n
    c_blk0 = (2 * di + g * n) // n
    w_x_blk0 = 0
    w_b_blk0 = di // n
    w_c_blk0 = (di + g * n) // n
    state_spec = pl.BlockSpec((1, 1, 2, h, n, hd), lambda b, i: (b, i, 0, 0, 0, 0))
    kern = functools.partial(_ssd_kernel, seq=seq)
    return pl.pallas_call(
        kern,
        grid=(bsz, g),
        in_specs=[
            pl.BlockSpec((1, seq, xw), lambda b, i: (b, 0, x_blk0 + i)),
            pl.BlockSpec((1, seq, n), lambda b, i: (b, 0, b_blk0 + i)),
            pl.BlockSpec((1, seq, n), lambda b, i: (b, 0, c_blk0 + i)),
            pl.BlockSpec((1, seq, xw), lambda b, i: (b, 0, i)),
            pl.BlockSpec((1, 1, 2, seq, h), lambda b, i: (b, i, 0, 0, 0)),
            pl.BlockSpec((1, 1, 2, h, seq), lambda b, i: (b, i, 0, 0, 0)),
            pl.BlockSpec((SSD_CONV, xw), lambda b, i: (0, w_x_blk0 + i)),
            pl.BlockSpec((SSD_CONV, n), lambda b, i: (0, w_b_blk0 + i)),
            pl.BlockSpec((SSD_CONV, n), lambda b, i: (0, w_c_blk0 + i)),
            pl.BlockSpec((1, xw), lambda b, i: (0, w_x_blk0 + i)),
            pl.BlockSpec((1, n), lambda b, i: (0, w_b_blk0 + i)),
            pl.BlockSpec((1, n), lambda b, i: (0, w_c_blk0 + i)),
            pl.BlockSpec((1, 2, 2, h), lambda b, i: (i, 0, 0, 0)),
            pl.BlockSpec((1, 2, 2, h, 1), lambda b, i: (i, 0, 0, 0, 0)),
            pl.BlockSpec((1, xw), lambda b, i: (0, i)),
            state_spec,
        ],
        out_specs=[pl.BlockSpec((1, seq, xw), lambda b, i: (b, 0, i)), state_spec],
        out_shape=[jax.ShapeDtypeStruct((bsz, seq, di), F32),
                   jax.ShapeDtypeStruct((bsz, g, 2, h, n, hd), F32)],
        scratch_shapes=[pltpu.VMEM((seq, xw), F32), pltpu.VMEM((seq, n), F32),
                        pltpu.VMEM((seq, n), F32), pltpu.VMEM((2, h, n, hd), F32),
                        pltpu.VMEM((seq + 16, xw), F32)],
        compiler_params=_params("parallel", "parallel"),
        name="ssd_scan",
    )(proj, proj, proj, proj, dt_cols, dt_rows, conv_w, conv_w, conv_w,
      conv_b, conv_b, conv_b, pc, pr, d_lanes, s0)


def _conf_kernel(top_ref, mid_ref, bot_ref, wdw_ref, bdw_ref, lnw_ref, lnb_ref,
                 w2_ref, b2_ref, res_ref, gate_ref, o_ref, win_ref, *, tm, halo):
    t = pl.program_id(1)
    nt = pl.num_programs(1)
    top = jnp.where(t > 0, top_ref[0], 0.0)
    bot = jnp.where(t < nt - 1, bot_ref[0], 0.0)
    win_ref[0:halo, :] = top
    win_ref[halo:halo + tm, :] = mid_ref[0]
    win_ref[halo + tm:2 * halo + tm, :] = bot
    lanes = 256
    ch = mid_ref.shape[2]
    pad = CONF_KERNEL // 2
    cols = []
    for c0 in range(0, ch, lanes):
        acc = jnp.zeros((tm, lanes), F32) + bdw_ref[:, c0:c0 + lanes]
        for k in range(CONF_KERNEL):
            off = halo - pad + k
            acc = acc + wdw_ref[k:k + 1, c0:c0 + lanes] * win_ref[off:off + tm, c0:c0 + lanes]
        cols.append(acc)
    hcv = jnp.concatenate(cols, axis=1)
    mu = jnp.mean(hcv, axis=-1, keepdims=True)
    dv = hcv - mu
    yn = dv * lax.rsqrt(jnp.mean(dv * dv, axis=-1, keepdims=True) + EPS)
    yn = _silu(yn * lnw_ref[...] + lnb_ref[...])
    out = jnp.dot(yn.astype(BF16), w2_ref[...], preferred_element_type=F32) + b2_ref[...]
    o_ref[0] = res_ref[0] + gate_ref[0] * out


def _conformer_tail(glu, w_dw, b_dw, ln_w, ln_b, w2, b2, res, gate, tm=128, halo=16):
    bsz, t, ch = glu.shape
    nh = tm // halo
    last_h = t // halo - 1
    row = lambda b, i: (b, i, 0)
    kern = functools.partial(_conf_kernel, tm=tm, halo=halo)
    return pl.pallas_call(
        kern,
        grid=(bsz, t // tm),
        in_specs=[
            pl.BlockSpec((1, halo, ch), lambda b, i: (b, jnp.maximum(i * nh - 1, 0), 0)),
            pl.BlockSpec((1, tm, ch), row),
            pl.BlockSpec((1, halo, ch), lambda b, i: (b, jnp.minimum((i + 1) * nh, last_h), 0)),
            _resident((CONF_KERNEL, ch)), _resident((1, ch)), _resident((1, ch)),
            _resident((1, ch)), _resident((ch, ch)), _resident((1, ch)),
            pl.BlockSpec((1, tm, ch), row),
            pl.BlockSpec((1, 1, ch), lambda b, i: (b, 0, 0)),
        ],
        out_specs=pl.BlockSpec((1, tm, ch), row),
        out_shape=jax.ShapeDtypeStruct((bsz, t, ch), F32),
        scratch_shapes=[pltpu.VMEM((tm + 2 * halo, ch), F32)],
        compiler_params=_params("parallel", "parallel"),
        name="conformer_tail",
    )(glu, glu, glu, w_dw, b_dw.reshape(1, ch), ln_w.reshape(1, ch), ln_b.reshape(1, ch),
      w2, b2.reshape(1, ch), res, gate)


def _ffn_kernel(*refs, tm, on_grid, final_norm, lanes):
    if final_norm:
        (top_ref, mid_ref, bot_ref, val_ref, cw_ref, cb_ref, wd_ref, res_ref, gate_ref,
         fn_ref, o_ref, win_ref, act_ref) = refs
    else:
        (top_ref, mid_ref, bot_ref, val_ref, cw_ref, cb_ref, wd_ref, res_ref, gate_ref,
         o_ref, win_ref, act_ref) = refs
        fn_ref = None
    halo = GRID_W
    t = pl.program_id(1)
    nt = pl.num_programs(1)
    hid = mid_ref.shape[2]
    base = FFN_GUARD + halo
    win_ref[0:FFN_GUARD, :] = jnp.zeros((FFN_GUARD, hid), F32)
    win_ref[base + tm + halo:base + tm + halo + FFN_GUARD, :] = jnp.zeros((FFN_GUARD, hid), F32)
    win_ref[FFN_GUARD:base, :] = jnp.where(t > 0, top_ref[0], 0.0).astype(F32)
    win_ref[base:base + tm, :] = mid_ref[0].astype(F32)
    win_ref[base + tm:base + tm + halo, :] = jnp.where(t < nt - 1, bot_ref[0], 0.0).astype(F32)
    col = lax.broadcasted_iota(jnp.int32, (tm, lanes), 0) % GRID_W
    row_taps = (-1, 0, 1) if on_grid else (0,)

    def body(j, carry):
        c0 = pl.multiple_of(j * lanes, lanes)
        cs = pl.ds(c0, lanes)
        parts = []
        for dc in (-1, 0, 1):
            acc = None
            for dr in row_taps:
                off = base + dr * GRID_W + dc
                w = cw_ref[(dr + 1) * FFN_CONV + (dc + 1), :, cs]
                term = w * win_ref[off:off + tm, cs]
                acc = term if acc is None else acc + term
            parts.append(acc)
        if on_grid:
            conv = (parts[1] + jnp.where(col >= 1, parts[0], 0.0)
                    + jnp.where(col <= GRID_W - 2, parts[2], 0.0))
        else:
            conv = parts[0] + parts[1] + parts[2]
        gate = conv + cb_ref[:, cs]
        act_ref[:, cs] = (_silu(gate) * val_ref[0, :, cs].astype(F32)).astype(BF16)
        return carry

    lax.fori_loop(0, hid // lanes, body, 0)
    out = jnp.dot(act_ref[...], wd_ref[...], preferred_element_type=F32)
    hn = res_ref[0] + gate_ref[0] * out
    if final_norm:
        hn = (hn * lax.rsqrt(jnp.mean(hn * hn, axis=-1, keepdims=True) + EPS)) * fn_ref[...]
    o_ref[0] = hn


def _ffn_tail(hu, conv_w, conv_b, w_down, res, gate, *, on_grid, final_norm_w=None,
              tm=256, lanes=256):
    bsz, t, two_hid = hu.shape
    hid = two_hid // 2
    d = w_down.shape[1]
    tm = min(tm, t)
    halo = GRID_W
    nh = tm // halo
    last_h = t // halo - 1
    row = lambda b, i: (b, i, 0)
    args = [hu, hu, hu, hu, conv_w.reshape(FFN_CONV * FFN_CONV, 1, hid),
            conv_b.reshape(1, hid), w_down, res, gate]
    specs = [
        pl.BlockSpec((1, halo, hid), lambda b, i: (b, jnp.maximum(i * nh - 1, 0), 1)),
        pl.BlockSpec((1, tm, hid), lambda b, i: (b, i, 1)),
        pl.BlockSpec((1, halo, hid), lambda b, i: (b, jnp.minimum((i + 1) * nh, last_h), 1)),
        pl.BlockSpec((1, tm, hid), lambda b, i: (b, i, 0)),
        _resident((FFN_CONV * FFN_CONV, 1, hid)), _resident((1, hid)), _resident((hid, d)),
        pl.BlockSpec((1, tm, d), row),
        pl.BlockSpec((1, 1, d), lambda b, i: (b, 0, 0)),
    ]
    if final_norm_w is not None:
        args.append(final_norm_w.reshape(1, d))
        specs.append(_resident((1, d)))
    kern = functools.partial(_ffn_kernel, tm=tm, on_grid=on_grid,
                             final_norm=final_norm_w is not None, lanes=lanes)
    return pl.pallas_call(
        kern,
        grid=(bsz, t // tm),
        in_specs=specs,
        out_specs=pl.BlockSpec((1, tm, d), row),
        out_shape=jax.ShapeDtypeStruct((bsz, t, d), F32),
        scratch_shapes=[pltpu.VMEM((tm + 2 * halo + 2 * FFN_GUARD, hid), F32),
                        pltpu.VMEM((tm, hid), BF16)],
        compiler_params=_params("parallel", "parallel"),
        name="ffn_tail",
    )(*args)


def _ssd_mixer(a_proj, ssd, s0):
    bsz, seq, _ = a_proj.shape
    g, h = SSD_GROUPS, SSD_HPG
    di = g * h * SSD_HEAD_DIM
    dt0 = 2 * di + 2 * g * SSD_STATE
    dt_raw = a_proj[:, :, dt0:dt0 + 2 * g * h].astype(F32).reshape(bsz, seq, 2, g, h)
    dt_cols = dt_raw.transpose(0, 3, 2, 1, 4)
    dt_rows = dt_raw.transpose(0, 3, 2, 4, 1)
    return _ssd_scan(a_proj, dt_cols, dt_rows, ssd["conv_w"], ssd["conv_b"],
                     ssd["pc"], ssd["pr"], ssd["d_lanes"], s0)


def kernel(x, c, ctx, c_ctx, mod_w, mod_b, norm1_w, norm2_w, ssd_w_in, ssd_conv_w, ssd_conv_b, ssd_dt_bias, ssd_a_log, ssd_d, ssd_norm_w, ssd_w_out, conf_w_pw1, conf_b_pw1, conf_w_dw, conf_b_dw, conf_ln_w, conf_ln_b, conf_w_pw2, conf_b_pw2, ffn_w_up, ffn_conv_w, ffn_conv_b, ffn_w_down, final_norm_w):
    bsz, seq, d = x.shape
    depth = mod_w.shape[0]
    g, h, hd, n = SSD_GROUPS, SSD_HPG, SSD_HEAD_DIM, SSD_STATE

    cond = jnp.concatenate([c, c_ctx[None, :], jnp.zeros((16 - bsz - 1, d), F32)], axis=0)
    mods = _ada_params(cond, mod_w, mod_b).reshape(depth, 16, 6, d)

    def latent_mod(i, k):
        return mods[i, :bsz, k][:, None, :]

    def context_mod(i, k):
        return jnp.broadcast_to(mods[i, bsz, k][None, None, :], (bsz, 1, d))

    hl, hc = x, ctx
    for i in range(depth):
        kind = i % 2
        j = i // 2
        last = i == depth - 1
        need_ctx = (not last) or kind == 0
        sh1, sc1, g1, sh2, sc2, g2 = (latent_mod(i, k) for k in range(6))
        if need_ctx:
            csh1, csc1, cg1, csh2, csc2, cg2 = (context_mod(i, k) for k in range(6))
        if kind == 0:
            w_in = ssd_w_in[j]
            w_in = jnp.pad(w_in, ((0, 0), (0, -w_in.shape[1] % 128))).astype(BF16)
            ssd = {
                "conv_w": ssd_conv_w[j],
                "conv_b": ssd_conv_b[j].reshape(1, -1),
                "pc": jnp.stack([ssd_dt_bias[j], ssd_a_log[j]]).reshape(2, 2, g, h).transpose(2, 0, 1, 3),
                "d_lanes": jnp.repeat(ssd_d[j], hd).reshape(1, g * h * hd),
            }
            ssd["pr"] = ssd["pc"][..., None]
            w_out = ssd_w_out[j].astype(BF16)
            zeros = jnp.zeros((bsz, g, 2, h, n, hd), F32)
            pc_ctx = _norm_matmul(hc, norm1_w[i], w_in, shift=csh1, scale=csc1, name="ssd_in_ctx")
            yc, s_ctx = _ssd_mixer(pc_ctx, ssd, zeros)
            pl_lat = _norm_matmul(hl, norm1_w[i], w_in, shift=sh1, scale=sc1, name="ssd_in")
            yl, _ = _ssd_mixer(pl_lat, ssd, s_ctx)
            hl = _norm_matmul(yl, ssd_norm_w[j], w_out, post="resid", res=hl, gate=g1,
                              name="ssd_out")
            if not last:
                hc = _norm_matmul(yc, ssd_norm_w[j], w_out, post="resid", res=hc, gate=cg1,
                                  name="ssd_out_ctx")
        else:
            w1 = conf_w_pw1[j].astype(BF16)
            glu = _norm_matmul(hl, norm1_w[i], w1, shift=sh1, scale=sc1, bias=conf_b_pw1[j],
                               post="glu", name="conf_pw1")
            hl_new = _conformer_tail(glu, conf_w_dw[j], conf_b_dw[j], conf_ln_w[j], conf_ln_b[j],
                                     conf_w_pw2[j].astype(BF16), conf_b_pw2[j], hl, g1)
            if not last:
                gluc = _norm_matmul(hc, norm1_w[i], w1, shift=csh1, scale=csc1,
                                    bias=conf_b_pw1[j], post="glu", name="conf_pw1_ctx")
                hc = _conformer_tail(gluc, conf_w_dw[j], conf_b_dw[j], conf_ln_w[j], conf_ln_b[j],
                                     conf_w_pw2[j].astype(BF16), conf_b_pw2[j], hc, cg1)
            hl = hl_new
        w_up = ffn_w_up[i].astype(BF16)
        w_down = ffn_w_down[i].astype(BF16)
        hu = _norm_matmul(hl, norm2_w[i], w_up, shift=sh2, scale=sc2, name="ffn_up")
        hl = _ffn_tail(hu, ffn_conv_w[i], ffn_conv_b[i], w_down, hl, g2, on_grid=True,
                       final_norm_w=final_norm_w if last else None)
        if not last:
            huc = _norm_matmul(hc, norm2_w[i], w_up, shift=csh2, scale=csc2, name="ffn_up_ctx")
            hc = _ffn_tail(huc, ffn_conv_w[i], ffn_conv_b[i], w_down, hc, cg2, on_grid=False)
    return hl
```

```python
import functools

import jax
import jax.numpy as jnp
from jax import lax
from jax.experimental import pallas as pl
from jax.experimental.pallas import tpu as pltpu

F32 = jnp.float32
BF16 = jnp.bfloat16

EPS = 1e-6
LANES = 128
SUBLANES = 8
GRID_W = 64
SSD_GROUPS = 8
SSD_HPG = 4
SSD_HEAD_DIM = 64
SSD_STATE = 128
SSD_CHUNK = 128
SSD_CONV = 5
CONF_KERNEL = 31
FFN_CONV = 3

VMEM_LIMIT_BYTES = 56 * 1024 * 1024


def _params(*semantics):
    return pltpu.CompilerParams(dimension_semantics=semantics,
                                vmem_limit_bytes=VMEM_LIMIT_BYTES)


def _sigmoid(v):
    return 0.5 * jnp.tanh(0.5 * v) + 0.5


def _silu(v):
    return v * _sigmoid(v)


def _softplus(v):
    return jnp.maximum(v, 0.0) + jnp.log1p(jnp.exp(-jnp.abs(v)))


def _split3(v):
    v1 = v.astype(BF16)
    r = v - v1.astype(F32)
    v2 = r.astype(BF16)
    r = r - v2.astype(F32)
    return v1, v2, r.astype(BF16)


def _rows_from(ext, start, n):
    total = ext.shape[0]
    if start % SUBLANES == 0:
        return ext[start:start + n]
    lo = (start // SUBLANES) * SUBLANES
    rolled = pltpu.roll(ext, (lo - start) % total, 0)
    return rolled[lo:lo + n]


def _resident(shape):
    nd = len(shape)
    return pl.BlockSpec(shape, lambda *_: (0,) * nd, pipeline_mode=pl.Buffered(1))


def _ada_kernel(cond_ref, w_ref, b_ref, o_ref):
    a = _silu(cond_ref[...]).astype(BF16)
    o_ref[0] = jnp.dot(a, w_ref[0].astype(BF16), preferred_element_type=F32) + b_ref[0]


def _ada_params(cond, mod_w, mod_b, tn=1536):
    depth, d, n = mod_w.shape
    rows = cond.shape[0]
    return pl.pallas_call(
        _ada_kernel,
        grid=(depth, n // tn),
        in_specs=[pl.BlockSpec((rows, d), lambda i, j: (0, 0)),
                  pl.BlockSpec((1, d, tn), lambda i, j: (i, 0, j)),
                  pl.BlockSpec((1, 1, tn), lambda i, j: (i, 0, j))],
        out_specs=pl.BlockSpec((1, rows, tn), lambda i, j: (i, 0, j)),
        out_shape=jax.ShapeDtypeStruct((depth, rows, n), F32),
        compiler_params=_params("parallel", "parallel"),
        name="ada_params",
    )(cond, mod_w, mod_b.reshape(depth, 1, n))


def _mm_kernel(*refs, modulate, post, has_bias, n_out, n_chunk):
    it = iter(refs)
    x_ref = next(it)
    g_ref = next(it)
    sh_ref = sc_ref = b_ref = res_ref = gate_ref = None
    if modulate:
        sh_ref, sc_ref = next(it), next(it)
    w_ref = next(it)
    if has_bias:
        b_ref = next(it)
    if post == "resid":
        res_ref, gate_ref = next(it), next(it)
    o_ref = next(it)

    x = x_ref[0].astype(F32)
    ms = jnp.mean(x * x, axis=-1, keepdims=True)
    y = (x * lax.rsqrt(ms + EPS)) * g_ref[...]
    if modulate:
        y = y * (1.0 + sc_ref[0]) + sh_ref[0]
    yb = y.astype(BF16)

    def proj(c0):
        acc = jnp.dot(yb, w_ref[:, c0:c0 + n_chunk], preferred_element_type=F32)
        if has_bias:
            acc = acc + b_ref[:, c0:c0 + n_chunk]
        return acc

    for n0 in range(0, n_out, n_chunk):
        if post == "glu":
            out = proj(n0) * _sigmoid(proj(n_out + n0))
        elif post == "resid":
            out = res_ref[0, :, n0:n0 + n_chunk] + gate_ref[0, :, n0:n0 + n_chunk] * proj(n0)
        else:
            out = proj(n0)
        o_ref[0, :, n0:n0 + n_chunk] = out.astype(o_ref.dtype)


def _norm_matmul(x, g, w, *, shift=None, scale=None, bias=None, post="store",
                 res=None, gate=None, out_dtype=F32, tm=512, n_chunk=1024, name=None):
    bsz, t, k = x.shape
    nw = w.shape[1]
    n_out = nw // 2 if post == "glu" else nw
    tm = min(tm, t)
    n_chunk = max(c for c in range(LANES, n_chunk + 1, LANES) if n_out % c == 0)
    assert t % tm == 0
    modulate = shift is not None
    row = lambda b, i: (b, i, 0)
    per_batch = lambda b, i: (b, 0, 0)
    args = [x, g.reshape(1, k)]
    specs = [pl.BlockSpec((1, tm, k), row), _resident((1, k))]
    if modulate:
        args += [shift, scale]
        specs += [pl.BlockSpec((1, 1, k), per_batch)] * 2
    args.append(w)
    specs.append(_resident((k, nw)))
    if bias is not None:
        args.append(bias.reshape(1, nw))
        specs.append(_resident((1, nw)))
    if post == "resid":
        args += [res, gate]
        specs += [pl.BlockSpec((1, tm, n_out), row), pl.BlockSpec((1, 1, n_out), per_batch)]
    kern = functools.partial(_mm_kernel, modulate=modulate, post=post,
                             has_bias=bias is not None, n_out=n_out, n_chunk=n_chunk)
    return pl.pallas_call(
        kern,
        grid=(bsz, t // tm),
        in_specs=specs,
        out_specs=pl.BlockSpec((1, tm, n_out), row),
        out_shape=jax.ShapeDtypeStruct((bsz, t, n_out), out_dtype),
        compiler_params=_params("parallel", "parallel"),
        name=name,
    )(*args)


def _dt_kernel(raw_ref, bias_ref, alog_ref, e_ref,
               cumc_ref, wc_ref, eac_ref, dtr_ref, cumr_ref, dec_ref, tot_ref, *, seq):
    q = SSD_CHUNK
    nc = seq // q
    ii = lax.broadcasted_iota(jnp.int32, (q, q), 0)
    jj = lax.broadcasted_iota(jnp.int32, (q, q), 1)
    lower_b = jnp.where(jj <= ii, 1.0, 0.0).astype(BF16)
    upper_b = jnp.where(jj >= ii, 1.0, 0.0).astype(BF16)
    lane = lax.broadcasted_iota(jnp.int32, (1, LANES), 1)
    backward = (lane // SSD_HPG) % 2 == 1
    bias = bias_ref[...]
    a_neg = -jnp.exp(alog_ref[...])

    def body(ci, carry):
        rows = pl.ds(pl.multiple_of(ci * q, q), q)
        dt = _softplus(raw_ref[0, rows, :].astype(F32) + bias)
        parts = _split3(dt * a_neg)
        cum_f = sum(jnp.dot(lower_b, p, preferred_element_type=F32) for p in parts)
        cum_b = sum(jnp.dot(upper_b, p, preferred_element_type=F32) for p in parts)
        cum = jnp.where(backward, cum_b, cum_f)
        tot = cum_f[q - 1:q, :]
        cumc_ref[0, rows, :] = cum
        wc_ref[0, rows, :] = (dt * jnp.exp(tot - cum)).astype(BF16)
        eac_ref[0, rows, :] = jnp.exp(cum).astype(BF16)
        dtr_ref[0, :, rows] = dt.T
        cumr_ref[0, :, rows] = cum.T
        tot_ref[pl.ds(ci, 1), :] = tot
        return carry

    lax.fori_loop(0, nc, body, 0)
    dec = jnp.exp(tot_ref[...])
    dec_ref[0] = sum(jnp.dot(p, e_ref[...], preferred_element_type=F32) for p in _split3(dec))


def _dt_factors(proj, bias, alog, expand):
    bsz, seq, width = proj.shape
    nc = seq // SSD_CHUNK
    blk = width // LANES - 1
    n_exp = expand.shape[1]
    col = pl.BlockSpec((1, seq, LANES), lambda b: (b, 0, 0))
    rowb = pl.BlockSpec((1, LANES, seq), lambda b: (b, 0, 0))
    return pl.pallas_call(
        functools.partial(_dt_kernel, seq=seq),
        grid=(bsz,),
        in_specs=[pl.BlockSpec((1, seq, LANES), lambda b: (b, 0, blk)),
                  _resident((1, LANES)), _resident((1, LANES)), _resident((LANES, n_exp))],
        out_specs=[col, col, col, rowb, rowb,
                   pl.BlockSpec((1, nc, n_exp), lambda b: (b, 0, 0))],
        out_shape=[jax.ShapeDtypeStruct((bsz, seq, LANES), F32),
                   jax.ShapeDtypeStruct((bsz, seq, LANES), BF16),
                   jax.ShapeDtypeStruct((bsz, seq, LANES), BF16),
                   jax.ShapeDtypeStruct((bsz, LANES, seq), F32),
                   jax.ShapeDtypeStruct((bsz, LANES, seq), F32),
                   jax.ShapeDtypeStruct((bsz, nc, n_exp), F32)],
        scratch_shapes=[pltpu.VMEM((nc, LANES), F32)],
        compiler_params=_params("parallel"),
        name="ssd_dt",
    )(proj, bias, alog, expand)


def _ssd_kernel(xp_ref, bp_ref, cp_ref, z_ref, cumc_ref, wc_ref, eac_ref, dtr_ref, cumr_ref,
                dec_ref, wx_ref, wb_ref, wcv_ref, bx_ref, bb_ref, bc_ref, d_ref, s0_ref,
                y_ref, sn_ref,
                cs_ref, sc_ref, sent_ref, ccol_ref, st_ref, e_ref, *, seq):
    q = SSD_CHUNK
    nc = seq // q
    nh, hd = SSD_HPG, SSD_HEAD_DIM
    xw = nh * hd
    g = pl.program_id(1)
    halo = SUBLANES

    kk = lax.broadcasted_iota(jnp.int32, (LANES, xw), 0)
    head_of_lane = lax.broadcasted_iota(jnp.int32, (LANES, xw), 1) // hd
    for d in range(2):
        e_ref[d] = jnp.where(kk == g * 2 * nh + d * nh + head_of_lane, 1.0, 0.0).astype(BF16)

    for gg in range(SSD_GROUPS):
        @pl.when(g == gg)
        def _():
            def cp_body(ci, carry):
                rows = pl.ds(pl.multiple_of(ci * q, q), q)
                ccol_ref[rows, :] = cumc_ref[0, rows, gg * 2 * nh:(gg + 1) * 2 * nh]
                return carry
            lax.fori_loop(0, nc, cp_body, 0, unroll=2)

    def conv_silu(src_ref, w_ref, b_ref, ci):
        r0 = pl.multiple_of(ci * q, q)
        up = pl.multiple_of(jnp.maximum(r0 - halo, 0), halo)
        dn = pl.multiple_of(jnp.minimum(r0 + q, seq - halo), halo)
        cur = src_ref[0, pl.ds(r0, q), :].astype(F32)
        top = jnp.where(ci > 0, src_ref[0, pl.ds(up, halo), :].astype(F32), 0.0)
        bot = jnp.where(ci < nc - 1, src_ref[0, pl.ds(dn, halo), :].astype(F32), 0.0)
        ext = jnp.concatenate([top, cur, bot], axis=0)
        w = w_ref[...]
        acc = b_ref[...] + w[SSD_CONV // 2:SSD_CONV // 2 + 1, :] * cur
        for k in range(SSD_CONV):
            if k != SSD_CONV // 2:
                acc = acc + w[k:k + 1, :] * _rows_from(ext, halo - SSD_CONV // 2 + k, q)
        return _silu(acc)

    ii = lax.broadcasted_iota(jnp.int32, (q, q), 0)
    jj = lax.broadcasted_iota(jnp.int32, (q, q), 1)
    lane_head = lax.broadcasted_iota(jnp.int32, (q, xw), 1) // hd

    def intra(ci, carry):
        rows = pl.ds(pl.multiple_of(ci * q, q), q)
        xc = conv_silu(xp_ref, wx_ref, bx_ref, ci)
        x_b = xc.astype(BF16)
        btb = conv_silu(bp_ref, wb_ref, bb_ref, ci).T.astype(BF16)
        cbf = conv_silu(cp_ref, wcv_ref, bc_ref, ci).astype(BF16)
        cs_ref[rows, :] = cbf
        cb = jnp.dot(cbf, btb, preferred_element_type=F32)
        cum8 = ccol_ref[rows, :]
        lm = []
        for h in range(nh):
            acf, acb = cum8[:, h:h + 1], cum8[:, nh + h:nh + h + 1]
            arf = cumr_ref[0, h:h + 1, rows]
            arb = cumr_ref[0, nh + h:nh + h + 1, rows]
            dtf = dtr_ref[0, h:h + 1, rows]
            dtb = dtr_ref[0, nh + h:nh + h + 1, rows]
            arg = jnp.where(jj <= ii, acf - arf, acb - arb)
            dts = jnp.where(jj < ii, dtf, jnp.where(jj > ii, dtb, dtf + dtb))
            lm.append((jnp.exp(arg) * (cb * dts)).astype(BF16))
        masked = [jnp.where(lane_head == h, x_b, jnp.zeros_like(x_b)) for h in range(nh)]
        yd = (jnp.dot(jnp.concatenate(lm[0:2], axis=1), jnp.concatenate(masked[0:2], axis=0),
                      preferred_element_type=F32)
              + jnp.dot(jnp.concatenate(lm[2:4], axis=1), jnp.concatenate(masked[2:4], axis=0),
                        preferred_element_type=F32))
        y_ref[0, rows, :] = d_ref[...] * xc + yd
        for d in range(2):
            wexp = jnp.dot(wc_ref[0, rows, :], e_ref[d], preferred_element_type=F32)
            sc_ref[ci, d] = jnp.dot(btb, (xc * wexp).astype(BF16), preferred_element_type=F32)
        return carry

    lax.fori_loop(0, nc, intra, 0, unroll=2)

    st_ref[...] = s0_ref[0, 0]

    def carry_states(c, carry):
        for d in range(2):
            ci = c if d == 0 else nc - 1 - c
            s_in = st_ref[d]
            sent_ref[ci, d] = s_in.astype(BF16)
            dec = dec_ref[0, pl.ds(ci, 1), d * xw:(d + 1) * xw]
            st_ref[d] = dec * s_in + sc_ref[ci, d]
        return carry

    lax.fori_loop(0, nc, carry_states, 0)
    sn_ref[0, 0] = st_ref[...]

    def inter(ci, carry):
        rows = pl.ds(pl.multiple_of(ci * q, q), q)
        cbf = cs_ref[rows, :]
        y = y_ref[0, rows, :]
        for d in range(2):
            ea = jnp.dot(eac_ref[0, rows, :], e_ref[d], preferred_element_type=F32)
            y = y + ea * jnp.dot(cbf, sent_ref[ci, d], preferred_element_type=F32)
        y_ref[0, rows, :] = (y * _silu(z_ref[0, rows, :].astype(F32))).astype(y_ref.dtype)
        return carry

    lax.fori_loop(0, nc, inter, 0, unroll=2)


def _ssd_scan(proj, dtf, conv_w, conv_b, d_lanes, s0):
    cumc, wc, eac, dtr, cumr, dec = dtf
    bsz, seq, _ = proj.shape
    g, h, hd, n = SSD_GROUPS, SSD_HPG, SSD_HEAD_DIM, SSD_STATE
    q = SSD_CHUNK
    nc = seq // q
    xw = h * hd
    di = g * xw
    x_blk0 = di // xw
    b_blk0 = (2 * di) // n
    c_blk0 = (2 * di + g * n) // n
    w_b_blk0 = di // n
    w_c_blk0 = (di + g * n) // n
    state_spec = pl.BlockSpec((1, 1, 2, n, xw), lambda b, i: (b, i, 0, 0, 0))
    col = pl.BlockSpec((1, seq, LANES), lambda b, i: (b, 0, 0))
    rowb = pl.BlockSpec((1, 2 * h, seq), lambda b, i: (b, i, 0))
    kern = functools.partial(_ssd_kernel, seq=seq)
    return pl.pallas_call(
        kern,
        grid=(bsz, g),
        in_specs=[
            pl.BlockSpec((1, seq, xw), lambda b, i: (b, 0, x_blk0 + i)),
            pl.BlockSpec((1, seq, n), lambda b, i: (b, 0, b_blk0 + i)),
            pl.BlockSpec((1, seq, n), lambda b, i: (b, 0, c_blk0 + i)),
            pl.BlockSpec((1, seq, xw), lambda b, i: (b, 0, i)),
            col, col, col, rowb, rowb,
            pl.BlockSpec((1, nc, 2 * xw), lambda b, i: (b, 0, i)),
            pl.BlockSpec((SSD_CONV, xw), lambda b, i: (0, i)),
            pl.BlockSpec((SSD_CONV, n), lambda b, i: (0, w_b_blk0 + i)),
            pl.BlockSpec((SSD_CONV, n), lambda b, i: (0, w_c_blk0 + i)),
            pl.BlockSpec((1, xw), lambda b, i: (0, i)),
            pl.BlockSpec((1, n), lambda b, i: (0, w_b_blk0 + i)),
            pl.BlockSpec((1, n), lambda b, i: (0, w_c_blk0 + i)),
            pl.BlockSpec((1, xw), lambda b, i: (0, i)),
            state_spec,
        ],
        out_specs=[pl.BlockSpec((1, seq, xw), lambda b, i: (b, 0, i)), state_spec],
        out_shape=[jax.ShapeDtypeStruct((bsz, seq, di), F32),
                   jax.ShapeDtypeStruct((bsz, g, 2, n, xw), F32)],
        scratch_shapes=[pltpu.VMEM((seq, n), BF16), pltpu.VMEM((nc, 2, n, xw), F32),
                        pltpu.VMEM((nc, 2, n, xw), BF16), pltpu.VMEM((seq, 2 * h), F32),
                        pltpu.VMEM((2, n, xw), F32), pltpu.VMEM((2, LANES, xw), BF16)],
        compiler_params=_params("parallel", "parallel"),
        name="ssd_scan",
    )(proj, proj, proj, proj, cumc, wc, eac, dtr, cumr, dec, conv_w, conv_w, conv_w,
      conv_b, conv_b, conv_b, d_lanes, s0)


def _conf_kernel(top_ref, mid_ref, bot_ref, wdw_ref, bdw_ref, lnw_ref, lnb_ref,
                 w2_ref, b2_ref, res_ref, gate_ref, o_ref, win_ref, cv_ref, *, tm, halo):
    t = pl.program_id(1)
    nt = pl.num_programs(1)
    win_ref[0:halo, :] = jnp.where(t > 0, top_ref[0], 0.0).astype(F32)
    win_ref[halo:halo + tm, :] = mid_ref[0].astype(F32)
    win_ref[halo + tm:2 * halo + tm, :] = jnp.where(t < nt - 1, bot_ref[0], 0.0).astype(F32)
    ch = mid_ref.shape[2]
    pad = CONF_KERNEL // 2
    ext_rows = tm + SUBLANES
    for c0 in range(0, ch, LANES):
        cs = slice(c0, c0 + LANES)
        acc = jnp.zeros((tm, LANES), F32) + bdw_ref[:, cs]
        for r in range(SUBLANES):
            part = None
            for a in range(-2, 2):
                k = SUBLANES * a + r + pad
                if 0 <= k < CONF_KERNEL:
                    start = halo + SUBLANES * a
                    term = wdw_ref[k:k + 1, cs] * win_ref[start:start + ext_rows, cs]
                    part = term if part is None else part + term
            acc = acc + _rows_from(part, r, tm)
        cv_ref[:, cs] = acc
    hcv = cv_ref[...]
    mu = jnp.mean(hcv, axis=-1, keepdims=True)
    dv = hcv - mu
    yn = dv * lax.rsqrt(jnp.mean(dv * dv, axis=-1, keepdims=True) + EPS)
    yn = _silu(yn * lnw_ref[...] + lnb_ref[...])
    out = jnp.dot(yn.astype(BF16), w2_ref[...], preferred_element_type=F32) + b2_ref[...]
    o_ref[0] = res_ref[0] + gate_ref[0] * out


def _conformer_tail(glu, w_dw, b_dw, ln_w, ln_b, w2, b2, res, gate, tm=128, halo=16):
    bsz, t, ch = glu.shape
    nh = tm // halo
    last_h = t // halo - 1
    row = lambda b, i: (b, i, 0)
    kern = functools.partial(_conf_kernel, tm=tm, halo=halo)
    return pl.pallas_call(
        kern,
        grid=(bsz, t // tm),
        in_specs=[
            pl.BlockSpec((1, halo, ch), lambda b, i: (b, jnp.maximum(i * nh - 1, 0), 0)),
            pl.BlockSpec((1, tm, ch), row),
            pl.BlockSpec((1, halo, ch), lambda b, i: (b, jnp.minimum((i + 1) * nh, last_h), 0)),
            _resident((CONF_KERNEL, ch)), _resident((1, ch)), _resident((1, ch)),
            _resident((1, ch)), _resident((ch, ch)), _resident((1, ch)),
            pl.BlockSpec((1, tm, ch), row),
            pl.BlockSpec((1, 1, ch), lambda b, i: (b, 0, 0)),
        ],
        out_specs=pl.BlockSpec((1, tm, ch), row),
        out_shape=jax.ShapeDtypeStruct((bsz, t, ch), F32),
        scratch_shapes=[pltpu.VMEM((tm + 2 * halo, ch), F32), pltpu.VMEM((tm, ch), F32)],
        compiler_params=_params("parallel", "parallel"),
        name="conformer_tail",
    )(glu, glu, glu, w_dw, b_dw.reshape(1, ch), ln_w.reshape(1, ch), ln_b.reshape(1, ch),
      w2, b2.reshape(1, ch), res, gate)


def _ffn_kernel(*refs, tm, on_grid, final_norm, lanes):
    if final_norm:
        (top_ref, mid_ref, bot_ref, val_ref, cw_ref, cb_ref, wd_ref, res_ref, gate_ref,
         fn_ref, o_ref, win_ref, act_ref) = refs
    else:
        (top_ref, mid_ref, bot_ref, val_ref, cw_ref, cb_ref, wd_ref, res_ref, gate_ref,
         o_ref, win_ref, act_ref) = refs
        fn_ref = None
    halo = GRID_W
    guard = SUBLANES
    t = pl.program_id(1)
    nt = pl.num_programs(1)
    hid = mid_ref.shape[2]
    base = guard + halo
    win_ref[0:guard, :] = jnp.zeros((guard, hid), F32)
    win_ref[base + tm + halo:base + tm + halo + guard, :] = jnp.zeros((guard, hid), F32)
    win_ref[guard:base, :] = jnp.where(t > 0, top_ref[0], 0.0).astype(F32)
    win_ref[base:base + tm, :] = mid_ref[0].astype(F32)
    win_ref[base + tm:base + tm + halo, :] = jnp.where(t < nt - 1, bot_ref[0], 0.0).astype(F32)
    col = lax.broadcasted_iota(jnp.int32, (tm, lanes), 0) % GRID_W
    row_taps = (-1, 0, 1) if on_grid else (0,)
    ext_rows = tm + 2 * guard

    def body(j, carry):
        c0 = pl.multiple_of(j * lanes, lanes)
        cs = pl.ds(c0, lanes)
        parts = []
        for dc in (-1, 0, 1):
            acc = None
            for dr in row_taps:
                start = base + dr * GRID_W - guard
                w = cw_ref[(dr + 1) * FFN_CONV + (dc + 1), :, cs]
                term = w * win_ref[start:start + ext_rows, cs]
                acc = term if acc is None else acc + term
            parts.append(_rows_from(acc, guard + dc, tm))
        if on_grid:
            conv = (parts[1] + jnp.where(col >= 1, parts[0], 0.0)
                    + jnp.where(col <= GRID_W - 2, parts[2], 0.0))
        else:
            conv = parts[0] + parts[1] + parts[2]
        gate = conv + cb_ref[:, cs]
        act_ref[:, cs] = (_silu(gate) * val_ref[0, :, cs].astype(F32)).astype(BF16)
        return carry

    lax.fori_loop(0, hid // lanes, body, 0)
    out = jnp.dot(act_ref[...], wd_ref[...], preferred_element_type=F32)
    hn = res_ref[0] + gate_ref[0] * out
    if final_norm:
        hn = (hn * lax.rsqrt(jnp.mean(hn * hn, axis=-1, keepdims=True) + EPS)) * fn_ref[...]
    o_ref[0] = hn


def _ffn_tail(hu, conv_w, conv_b, w_down, res, gate, *, on_grid, final_norm_w=None,
              tm=256, lanes=256):
    bsz, t, two_hid = hu.shape
    hid = two_hid // 2
    d = w_down.shape[1]
    tm = min(tm, t)
    halo = GRID_W
    nh = tm // halo
    last_h = t // halo - 1
    row = lambda b, i: (b, i, 0)
    args = [hu, hu, hu, hu, conv_w.reshape(FFN_CONV * FFN_CONV, 1, hid),
            conv_b.reshape(1, hid), w_down, res, gate]
    specs = [
        pl.BlockSpec((1, halo, hid), lambda b, i: (b, jnp.maximum(i * nh - 1, 0), 1)),
        pl.BlockSpec((1, tm, hid), lambda b, i: (b, i, 1)),
        pl.BlockSpec((1, halo, hid), lambda b, i: (b, jnp.minimum((i + 1) * nh, last_h), 1)),
        pl.BlockSpec((1, tm, hid), lambda b, i: (b, i, 0)),
        _resident((FFN_CONV * FFN_CONV, 1, hid)), _resident((1, hid)), _resident((hid, d)),
        pl.BlockSpec((1, tm, d), row),
        pl.BlockSpec((1, 1, d), lambda b, i: (b, 0, 0)),
    ]
    if final_norm_w is not None:
        args.append(final_norm_w.reshape(1, d))
        specs.append(_resident((1, d)))
    kern = functools.partial(_ffn_kernel, tm=tm, on_grid=on_grid,
                             final_norm=final_norm_w is not None, lanes=lanes)
    return pl.pallas_call(
        kern,
        grid=(bsz, t // tm),
        in_specs=specs,
        out_specs=pl.BlockSpec((1, tm, d), row),
        out_shape=jax.ShapeDtypeStruct((bsz, t, d), F32),
        scratch_shapes=[pltpu.VMEM((tm + 2 * halo + 2 * SUBLANES, hid), F32),
                        pltpu.VMEM((tm, hid), BF16)],
        compiler_params=_params("parallel", "parallel"),
        name="ffn_tail",
    )(*args)


def _group_dir_head(v):
    lead = v.shape[:-1]
    v = v.reshape(lead + (2, SSD_GROUPS, SSD_HPG))
    return jnp.swapaxes(v, -3, -2).reshape(lead + (2 * SSD_GROUPS * SSD_HPG,))


def kernel(x, c, ctx, c_ctx, mod_w, mod_b, norm1_w, norm2_w, ssd_w_in, ssd_conv_w, ssd_conv_b, ssd_dt_bias, ssd_a_log, ssd_d, ssd_norm_w, ssd_w_out, conf_w_pw1, conf_b_pw1, conf_w_dw, conf_b_dw, conf_ln_w, conf_ln_b, conf_w_pw2, conf_b_pw2, ffn_w_up, ffn_conv_w, ffn_conv_b, ffn_w_down, final_norm_w):
    bsz, seq, d = x.shape
    depth = mod_w.shape[0]
    g, h, hd, n = SSD_GROUPS, SSD_HPG, SSD_HEAD_DIM, SSD_STATE
    n_heads2 = 2 * g * h

    cond = jnp.concatenate([c, c_ctx[None, :], jnp.zeros((16 - bsz - 1, d), F32)], axis=0)
    mods = _ada_params(cond, mod_w, mod_b).reshape(depth, 16, 6, d)

    def latent_mod(i, k):
        return mods[i, :bsz, k][:, None, :]

    def context_mod(i, k):
        return jnp.broadcast_to(mods[i, bsz, k][None, None, :], (bsz, 1, d))

    hl, hc = x, ctx
    for i in range(depth):
        kind = i % 2
        j = i // 2
        last = i == depth - 1
        need_ctx = (not last) or kind == 0
        sh1, sc1, g1, sh2, sc2, g2 = (latent_mod(i, k) for k in range(6))
        if need_ctx:
            csh1, csc1, cg1, csh2, csc2, cg2 = (context_mod(i, k) for k in range(6))
        if kind == 0:
            w_in = ssd_w_in[j]
            dt0 = w_in.shape[1] - n_heads2
            w_in = jnp.concatenate(
                [w_in[:, :dt0], _group_dir_head(w_in[:, dt0:]),
                 jnp.zeros((d, LANES - n_heads2), F32)], axis=1).astype(BF16)
            lane_pad = lambda v: jnp.pad(_group_dir_head(v.reshape(-1)),
                                         (0, LANES - n_heads2)).reshape(1, LANES)
            dt_bias, a_log = lane_pad(ssd_dt_bias[j]), lane_pad(ssd_a_log[j])
            expand = (jnp.arange(LANES)[:, None] == jnp.arange(n_heads2 * hd)[None, :] // hd
                      ).astype(BF16)
            conv_b = ssd_conv_b[j].reshape(1, -1)
            d_lanes = jnp.repeat(ssd_d[j], hd).reshape(1, g * h * hd)
            w_out = ssd_w_out[j].astype(BF16)

            def mixer(hin, shift, scale, s0, name):
                proj = _norm_matmul(hin, norm1_w[i], w_in, shift=shift, scale=scale, name=name)
                dtf = _dt_factors(proj, dt_bias, a_log, expand)
                return _ssd_scan(proj, dtf, ssd_conv_w[j], conv_b, d_lanes, s0)

            zeros = jnp.zeros((bsz, g, 2, n, h * hd), F32)
            yc, s_ctx = mixer(hc, csh1, csc1, zeros, "ssd_in_ctx")
            yl, _ = mixer(hl, sh1, sc1, s_ctx, "ssd_in")
            hl = _norm_matmul(yl, ssd_norm_w[j], w_out, post="resid", res=hl, gate=g1,
                              name="ssd_out")
            if not last:
                hc = _norm_matmul(yc, ssd_norm_w[j], w_out, post="resid", res=hc, gate=cg1,
                                  name="ssd_out_ctx")
        else:
            w1 = conf_w_pw1[j].astype(BF16)
            glu = _norm_matmul(hl, norm1_w[i], w1, shift=sh1, scale=sc1, bias=conf_b_pw1[j],
                               post="glu", name="conf_pw1")
            hl_new = _conformer_tail(glu, conf_w_dw[j], conf_b_dw[j], conf_ln_w[j], conf_ln_b[j],
                                     conf_w_pw2[j].astype(BF16), conf_b_pw2[j], hl, g1)
            if not last:
                gluc = _norm_matmul(hc, norm1_w[i], w1, shift=csh1, scale=csc1,
                                    bias=conf_b_pw1[j], post="glu", name="conf_pw1_ctx")
                hc = _conformer_tail(gluc, conf_w_dw[j], conf_b_dw[j], conf_ln_w[j], conf_ln_b[j],
                                     conf_w_pw2[j].astype(BF16), conf_b_pw2[j], hc, cg1)
            hl = hl_new
        w_up = ffn_w_up[i].astype(BF16)
        w_down = ffn_w_down[i].astype(BF16)
        hu = _norm_matmul(hl, norm2_w[i], w_up, shift=sh2, scale=sc2, name="ffn_up")
        hl = _ffn_tail(hu, ffn_conv_w[i], ffn_conv_b[i], w_down, hl, g2, on_grid=True,
                       final_norm_w=final_norm_w if last else None)
        if not last:
            huc = _norm_matmul(hc, norm2_w[i], w_up, shift=csh2, scale=csc2, name="ffn_up_ctx")
            hc = _ffn_tail(huc, ffn_conv_w[i], ffn_conv_b[i], w_down, hc, cg2, on_grid=False)
    return hl
```

```python
import functools

import jax
import jax.numpy as jnp
from jax import lax
from jax.experimental import pallas as pl
from jax.experimental.pallas import tpu as pltpu

F32 = jnp.float32
BF16 = jnp.bfloat16

EPS = 1e-6
LANES = 128
SUBLANES = 8
GRID_W = 64
SSD_GROUPS = 8
SSD_HPG = 4
SSD_HEAD_DIM = 64
SSD_STATE = 128
SSD_CHUNK = 128
SSD_CONV = 5
CONF_KERNEL = 31
FFN_CONV = 3

VMEM_LIMIT_BYTES = 56 * 1024 * 1024


def _params(*semantics):
    return pltpu.CompilerParams(dimension_semantics=semantics,
                                vmem_limit_bytes=VMEM_LIMIT_BYTES)


def _sigmoid(v):
    return 0.5 * jnp.tanh(0.5 * v) + 0.5


def _silu(v):
    return v * _sigmoid(v)


def _softplus(v):
    return jnp.maximum(v, 0.0) + jnp.log1p(jnp.exp(-jnp.abs(v)))


def _split3(v):
    v1 = v.astype(BF16)
    r = v - v1.astype(F32)
    v2 = r.astype(BF16)
    r = r - v2.astype(F32)
    return v1, v2, r.astype(BF16)


def _rows_from(ext, start, n):
    total = ext.shape[0]
    if start % SUBLANES == 0:
        return ext[start:start + n]
    lo = (start // SUBLANES) * SUBLANES
    rolled = pltpu.roll(ext, (lo - start) % total, 0)
    return rolled[lo:lo + n]


def _resident(shape):
    nd = len(shape)
    return pl.BlockSpec(shape, lambda *_: (0,) * nd, pipeline_mode=pl.Buffered(1))


def _ada_kernel(cond_ref, w_ref, b_ref, o_ref):
    a = _silu(cond_ref[...]).astype(BF16)
    o_ref[0] = jnp.dot(a, w_ref[0].astype(BF16), preferred_element_type=F32) + b_ref[0]


def _ada_params(cond, mod_w, mod_b, tn=1536):
    depth, d, n = mod_w.shape
    rows = cond.shape[0]
    return pl.pallas_call(
        _ada_kernel,
        grid=(depth, n // tn),
        in_specs=[pl.BlockSpec((rows, d), lambda i, j: (0, 0)),
                  pl.BlockSpec((1, d, tn), lambda i, j: (i, 0, j)),
                  pl.BlockSpec((1, 1, tn), lambda i, j: (i, 0, j))],
        out_specs=pl.BlockSpec((1, rows, tn), lambda i, j: (i, 0, j)),
        out_shape=jax.ShapeDtypeStruct((depth, rows, n), F32),
        compiler_params=_params("parallel", "parallel"),
        name="ada_params",
    )(cond, mod_w, mod_b.reshape(depth, 1, n))


def _mm_kernel(*refs, modulate, post, has_bias, n_out, n_chunk):
    it = iter(refs)
    x_ref = next(it)
    g_ref = next(it)
    sh_ref = sc_ref = b_ref = res_ref = gate_ref = None
    if modulate:
        sh_ref, sc_ref = next(it), next(it)
    w_ref = next(it)
    if has_bias:
        b_ref = next(it)
    if post == "resid":
        res_ref, gate_ref = next(it), next(it)
    o_ref = next(it)

    x = x_ref[0].astype(F32)
    ms = jnp.mean(x * x, axis=-1, keepdims=True)
    y = (x * lax.rsqrt(ms + EPS)) * g_ref[...]
    if modulate:
        y = y * (1.0 + sc_ref[0]) + sh_ref[0]
    yb = y.astype(BF16)

    def proj(c0):
        acc = jnp.dot(yb, w_ref[:, c0:c0 + n_chunk], preferred_element_type=F32)
        if has_bias:
            acc = acc + b_ref[:, c0:c0 + n_chunk]
        return acc

    for n0 in range(0, n_out, n_chunk):
        if post == "glu":
            out = proj(n0) * _sigmoid(proj(n_out + n0))
        elif post == "resid":
            out = res_ref[0, :, n0:n0 + n_chunk] + gate_ref[0, :, n0:n0 + n_chunk] * proj(n0)
        else:
            out = proj(n0)
        o_ref[0, :, n0:n0 + n_chunk] = out.astype(o_ref.dtype)


def _norm_matmul(x, g, w, *, shift=None, scale=None, bias=None, post="store",
                 res=None, gate=None, out_dtype=F32, tm=512, n_chunk=1024, name=None):
    bsz, t, k = x.shape
    nw = w.shape[1]
    n_out = nw // 2 if post == "glu" else nw
    tm = min(tm, t)
    n_chunk = max(c for c in range(LANES, n_chunk + 1, LANES) if n_out % c == 0)
    assert t % tm == 0
    modulate = shift is not None
    row = lambda b, i: (b, i, 0)
    per_batch = lambda b, i: (b, 0, 0)
    args = [x, g.reshape(1, k)]
    specs = [pl.BlockSpec((1, tm, k), row), _resident((1, k))]
    if modulate:
        args += [shift, scale]
        specs += [pl.BlockSpec((1, 1, k), per_batch)] * 2
    args.append(w)
    specs.append(_resident((k, nw)))
    if bias is not None:
        args.append(bias.reshape(1, nw))
        specs.append(_resident((1, nw)))
    if post == "resid":
        args += [res, gate]
        specs += [pl.BlockSpec((1, tm, n_out), row), pl.BlockSpec((1, 1, n_out), per_batch)]
    kern = functools.partial(_mm_kernel, modulate=modulate, post=post,
                             has_bias=bias is not None, n_out=n_out, n_chunk=n_chunk)
    return pl.pallas_call(
        kern,
        grid=(bsz, t // tm),
        in_specs=specs,
        out_specs=pl.BlockSpec((1, tm, n_out), row),
        out_shape=jax.ShapeDtypeStruct((bsz, t, n_out), out_dtype),
        compiler_params=_params("parallel", "parallel"),
        name=name,
    )(*args)


def _dt_kernel(raw_ref, bias_ref, alog_ref, e_ref,
               cumc_ref, wc_ref, eac_ref, dtr_ref, cumr_ref, dec_ref, tot_ref, *, seq):
    q = SSD_CHUNK
    nc = seq // q
    ii = lax.broadcasted_iota(jnp.int32, (q, q), 0)
    jj = lax.broadcasted_iota(jnp.int32, (q, q), 1)
    lower_b = jnp.where(jj <= ii, 1.0, 0.0).astype(BF16)
    upper_b = jnp.where(jj >= ii, 1.0, 0.0).astype(BF16)
    lane = lax.broadcasted_iota(jnp.int32, (1, LANES), 1)
    backward = (lane // SSD_HPG) % 2 == 1
    bias = bias_ref[...]
    a_neg = -jnp.exp(alog_ref[...])

    def body(ci, carry):
        rows = pl.ds(pl.multiple_of(ci * q, q), q)
        dt = _softplus(raw_ref[0, rows, :].astype(F32) + bias)
        parts = _split3(dt * a_neg)
        cum_f = sum(jnp.dot(lower_b, p, preferred_element_type=F32) for p in parts)
        cum_b = sum(jnp.dot(upper_b, p, preferred_element_type=F32) for p in parts)
        cum = jnp.where(backward, cum_b, cum_f)
        tot = cum_f[q - 1:q, :]
        cumc_ref[0, rows, :] = cum
        wc_ref[0, rows, :] = (dt * jnp.exp(tot - cum)).astype(BF16)
        eac_ref[0, rows, :] = jnp.exp(cum).astype(BF16)
        dtr_ref[0, :, rows] = dt.T
        cumr_ref[0, :, rows] = cum.T
        tot_ref[pl.ds(ci, 1), :] = tot
        return carry

    lax.fori_loop(0, nc, body, 0)
    dec = jnp.exp(tot_ref[...])
    dec_ref[0] = sum(jnp.dot(p, e_ref[...], preferred_element_type=F32) for p in _split3(dec))


def _dt_factors(proj, bias, alog, expand):
    bsz, seq, width = proj.shape
    nc = seq // SSD_CHUNK
    blk = width // LANES - 1
    n_exp = expand.shape[1]
    col = pl.BlockSpec((1, seq, LANES), lambda b: (b, 0, 0))
    rowb = pl.BlockSpec((1, LANES, seq), lambda b: (b, 0, 0))
    return pl.pallas_call(
        functools.partial(_dt_kernel, seq=seq),
        grid=(bsz,),
        in_specs=[pl.BlockSpec((1, seq, LANES), lambda b: (b, 0, blk)),
                  _resident((1, LANES)), _resident((1, LANES)), _resident((LANES, n_exp))],
        out_specs=[col, col, col, rowb, rowb,
                   pl.BlockSpec((1, nc, n_exp), lambda b: (b, 0, 0))],
        out_shape=[jax.ShapeDtypeStruct((bsz, seq, LANES), F32),
                   jax.ShapeDtypeStruct((bsz, seq, LANES), BF16),
                   jax.ShapeDtypeStruct((bsz, seq, LANES), BF16),
                   jax.ShapeDtypeStruct((bsz, LANES, seq), F32),
                   jax.ShapeDtypeStruct((bsz, LANES, seq), F32),
                   jax.ShapeDtypeStruct((bsz, nc, n_exp), F32)],
        scratch_shapes=[pltpu.VMEM((nc, LANES), F32)],
        compiler_params=_params("parallel"),
        name="ssd_dt",
    )(proj, bias, alog, expand)


def _ssd_kernel(xp_ref, bp_ref, cp_ref, z_ref, cumc_ref, wc_ref, eac_ref, dtr_ref, cumr_ref,
                dec_ref, wx_ref, wb_ref, wcv_ref, bx_ref, bb_ref, bc_ref, d_ref, s0_ref,
                y_ref, sn_ref,
                cs_ref, sc_ref, sent_ref, ccol_ref, st_ref, e_ref, *, seq):
    q = SSD_CHUNK
    nc = seq // q
    nh, hd = SSD_HPG, SSD_HEAD_DIM
    xw = nh * hd
    g = pl.program_id(1)
    halo = SUBLANES

    kk = lax.broadcasted_iota(jnp.int32, (LANES, xw), 0)
    head_of_lane = lax.broadcasted_iota(jnp.int32, (LANES, xw), 1) // hd
    for d in range(2):
        e_ref[d] = jnp.where(kk == g * 2 * nh + d * nh + head_of_lane, 1.0, 0.0).astype(BF16)

    for gg in range(SSD_GROUPS):
        @pl.when(g == gg)
        def _():
            def cp_body(ci, carry):
                rows = pl.ds(pl.multiple_of(ci * q, q), q)
                ccol_ref[rows, :] = cumc_ref[0, rows, gg * 2 * nh:(gg + 1) * 2 * nh]
                return carry
            lax.fori_loop(0, nc, cp_body, 0, unroll=2)

    def conv_silu(src_ref, w_ref, b_ref, ci):
        r0 = pl.multiple_of(ci * q, q)
        up = pl.multiple_of(jnp.maximum(r0 - halo, 0), halo)
        dn = pl.multiple_of(jnp.minimum(r0 + q, seq - halo), halo)
        cur = src_ref[0, pl.ds(r0, q), :].astype(F32)
        top = jnp.where(ci > 0, src_ref[0, pl.ds(up, halo), :].astype(F32), 0.0)
        bot = jnp.where(ci < nc - 1, src_ref[0, pl.ds(dn, halo), :].astype(F32), 0.0)
        ext = jnp.concatenate([top, cur, bot], axis=0)
        w = w_ref[...]
        acc = b_ref[...] + w[SSD_CONV // 2:SSD_CONV // 2 + 1, :] * cur
        for k in range(SSD_CONV):
            if k != SSD_CONV // 2:
                acc = acc + w[k:k + 1, :] * _rows_from(ext, halo - SSD_CONV // 2 + k, q)
        return _silu(acc)

    ii = lax.broadcasted_iota(jnp.int32, (q, q), 0)
    jj = lax.broadcasted_iota(jnp.int32, (q, q), 1)
    lane_head = lax.broadcasted_iota(jnp.int32, (q, xw), 1) // hd

    def intra(ci, carry):
        rows = pl.ds(pl.multiple_of(ci * q, q), q)
        xc = conv_silu(xp_ref, wx_ref, bx_ref, ci)
        x_b = xc.astype(BF16)
        btb = conv_silu(bp_ref, wb_ref, bb_ref, ci).T.astype(BF16)
        cbf = conv_silu(cp_ref, wcv_ref, bc_ref, ci).astype(BF16)
        cs_ref[rows, :] = cbf
        cb = jnp.dot(cbf, btb, preferred_element_type=F32)
        cum8 = ccol_ref[rows, :]
        lm = []
        for h in range(nh):
            acf, acb = cum8[:, h:h + 1], cum8[:, nh + h:nh + h + 1]
            arf = cumr_ref[0, h:h + 1, rows]
            arb = cumr_ref[0, nh + h:nh + h + 1, rows]
            dtf = dtr_ref[0, h:h + 1, rows]
            dtb = dtr_ref[0, nh + h:nh + h + 1, rows]
            arg = jnp.where(jj <= ii, acf - arf, acb - arb)
            dts = jnp.where(jj < ii, dtf, jnp.where(jj > ii, dtb, dtf + dtb))
            lm.append((jnp.exp(arg) * (cb * dts)).astype(BF16))
        masked = [jnp.where(lane_head == h, x_b, jnp.zeros_like(x_b)) for h in range(nh)]
        yd = (jnp.dot(jnp.concatenate(lm[0:2], axis=1), jnp.concatenate(masked[0:2], axis=0),
                      preferred_element_type=F32)
              + jnp.dot(jnp.concatenate(lm[2:4], axis=1), jnp.concatenate(masked[2:4], axis=0),
                        preferred_element_type=F32))
        y_ref[0, rows, :] = d_ref[...] * xc + yd
        for d in range(2):
            wexp = jnp.dot(wc_ref[0, rows, :], e_ref[d], preferred_element_type=F32)
            sc_ref[ci, d] = jnp.dot(btb, (xc * wexp).astype(BF16), preferred_element_type=F32)
        return carry

    lax.fori_loop(0, nc, intra, 0, unroll=2)

    st_ref[...] = s0_ref[0, 0]

    def carry_states(c, carry):
        for d in range(2):
            ci = c if d == 0 else nc - 1 - c
            s_in = st_ref[d]
            sent_ref[ci, d] = s_in.astype(BF16)
            dec = dec_ref[0, pl.ds(ci, 1), d * xw:(d + 1) * xw]
            st_ref[d] = dec * s_in + sc_ref[ci, d]
        return carry

    lax.fori_loop(0, nc, carry_states, 0)
    sn_ref[0, 0] = st_ref[...]

    def inter(ci, carry):
        rows = pl.ds(pl.multiple_of(ci * q, q), q)
        cbf = cs_ref[rows, :]
        y = y_ref[0, rows, :]
        for d in range(2):
            ea = jnp.dot(eac_ref[0, rows, :], e_ref[d], preferred_element_type=F32)
            y = y + ea * jnp.dot(cbf, sent_ref[ci, d], preferred_element_type=F32)
        y_ref[0, rows, :] = (y * _silu(z_ref[0, rows, :].astype(F32))).astype(y_ref.dtype)
        return carry

    lax.fori_loop(0, nc, inter, 0, unroll=2)


def _ssd_scan(proj, dtf, conv_w, conv_b, d_lanes, s0):
    cumc, wc, eac, dtr, cumr, dec = dtf
    bsz, seq, _ = proj.shape
    g, h, hd, n = SSD_GROUPS, SSD_HPG, SSD_HEAD_DIM, SSD_STATE
    q = SSD_CHUNK
    nc = seq // q
    xw = h * hd
    di = g * xw
    x_blk0 = di // xw
    b_blk0 = (2 * di) // n
    c_blk0 = (2 * di + g * n) // n
    w_b_blk0 = di // n
    w_c_blk0 = (di + g * n) // n
    state_spec = pl.BlockSpec((1, 1, 2, n, xw), lambda b, i: (b, i, 0, 0, 0))
    col = pl.BlockSpec((1, seq, LANES), lambda b, i: (b, 0, 0))
    rowb = pl.BlockSpec((1, 2 * h, seq), lambda b, i: (b, i, 0))
    kern = functools.partial(_ssd_kernel, seq=seq)
    return pl.pallas_call(
        kern,
        grid=(bsz, g),
        in_specs=[
            pl.BlockSpec((1, seq, xw), lambda b, i: (b, 0, x_blk0 + i)),
            pl.BlockSpec((1, seq, n), lambda b, i: (b, 0, b_blk0 + i)),
            pl.BlockSpec((1, seq, n), lambda b, i: (b, 0, c_blk0 + i)),
            pl.BlockSpec((1, seq, xw), lambda b, i: (b, 0, i)),
            col, col, col, rowb, rowb,
            pl.BlockSpec((1, nc, 2 * xw), lambda b, i: (b, 0, i)),
            pl.BlockSpec((SSD_CONV, xw), lambda b, i: (0, i)),
            pl.BlockSpec((SSD_CONV, n), lambda b, i: (0, w_b_blk0 + i)),
            pl.BlockSpec((SSD_CONV, n), lambda b, i: (0, w_c_blk0 + i)),
            pl.BlockSpec((1, xw), lambda b, i: (0, i)),
            pl.BlockSpec((1, n), lambda b, i: (0, w_b_blk0 + i)),
            pl.BlockSpec((1, n), lambda b, i: (0, w_c_blk0 + i)),
            pl.BlockSpec((1, xw), lambda b, i: (0, i)),
            state_spec,
        ],
        out_specs=[pl.BlockSpec((1, seq, xw), lambda b, i: (b, 0, i)), state_spec],
        out_shape=[jax.ShapeDtypeStruct((bsz, seq, di), F32),
                   jax.ShapeDtypeStruct((bsz, g, 2, n, xw), F32)],
        scratch_shapes=[pltpu.VMEM((seq, n), BF16), pltpu.VMEM((nc, 2, n, xw), F32),
                        pltpu.VMEM((nc, 2, n, xw), BF16), pltpu.VMEM((seq, 2 * h), F32),
                        pltpu.VMEM((2, n, xw), F32), pltpu.VMEM((2, LANES, xw), BF16)],
        compiler_params=_params("parallel", "parallel"),
        name="ssd_scan",
    )(proj, proj, proj, proj, cumc, wc, eac, dtr, cumr, dec, conv_w, conv_w, conv_w,
      conv_b, conv_b, conv_b, d_lanes, s0)


def _conf_kernel(top_ref, mid_ref, bot_ref, wdw_ref, bdw_ref, lnw_ref, lnb_ref,
                 w2_ref, b2_ref, res_ref, gate_ref, o_ref, win_ref, cv_ref, *, tm, halo):
    t = pl.program_id(1)
    nt = pl.num_programs(1)
    win_ref[0:halo, :] = jnp.where(t > 0, top_ref[0], 0.0).astype(F32)
    win_ref[halo:halo + tm, :] = mid_ref[0].astype(F32)
    win_ref[halo + tm:2 * halo + tm, :] = jnp.where(t < nt - 1, bot_ref[0], 0.0).astype(F32)
    ch = mid_ref.shape[2]
    pad = CONF_KERNEL // 2
    ext_rows = tm + SUBLANES
    for c0 in range(0, ch, LANES):
        cs = slice(c0, c0 + LANES)
        acc = jnp.zeros((tm, LANES), F32) + bdw_ref[:, cs]
        for r in range(SUBLANES):
            part = None
            for a in range(-2, 2):
                k = SUBLANES * a + r + pad
                if 0 <= k < CONF_KERNEL:
                    start = halo + SUBLANES * a
                    term = wdw_ref[k:k + 1, cs] * win_ref[start:start + ext_rows, cs]
                    part = term if part is None else part + term
            acc = acc + _rows_from(part, r, tm)
        cv_ref[:, cs] = acc
    hcv = cv_ref[...]
    mu = jnp.mean(hcv, axis=-1, keepdims=True)
    dv = hcv - mu
    yn = dv * lax.rsqrt(jnp.mean(dv * dv, axis=-1, keepdims=True) + EPS)
    yn = _silu(yn * lnw_ref[...] + lnb_ref[...])
    out = jnp.dot(yn.astype(BF16), w2_ref[...], preferred_element_type=F32) + b2_ref[...]
    o_ref[0] = res_ref[0] + gate_ref[0] * out


def _conformer_tail(glu, w_dw, b_dw, ln_w, ln_b, w2, b2, res, gate, tm=128, halo=16):
    bsz, t, ch = glu.shape
    nh = tm // halo
    last_h = t // halo - 1
    row = lambda b, i: (b, i, 0)
    kern = functools.partial(_conf_kernel, tm=tm, halo=halo)
    return pl.pallas_call(
        kern,
        grid=(bsz, t // tm),
        in_specs=[
            pl.BlockSpec((1, halo, ch), lambda b, i: (b, jnp.maximum(i * nh - 1, 0), 0)),
            pl.BlockSpec((1, tm, ch), row),
            pl.BlockSpec((1, halo, ch), lambda b, i: (b, jnp.minimum((i + 1) * nh, last_h), 0)),
            _resident((CONF_KERNEL, ch)), _resident((1, ch)), _resident((1, ch)),
            _resident((1, ch)), _resident((ch, ch)), _resident((1, ch)),
            pl.BlockSpec((1, tm, ch), row),
            pl.BlockSpec((1, 1, ch), lambda b, i: (b, 0, 0)),
        ],
        out_specs=pl.BlockSpec((1, tm, ch), row),
        out_shape=jax.ShapeDtypeStruct((bsz, t, ch), F32),
        scratch_shapes=[pltpu.VMEM((tm + 2 * halo, ch), F32), pltpu.VMEM((tm, ch), F32)],
        compiler_params=_params("parallel", "parallel"),
        name="conformer_tail",
    )(glu, glu, glu, w_dw, b_dw.reshape(1, ch), ln_w.reshape(1, ch), ln_b.reshape(1, ch),
      w2, b2.reshape(1, ch), res, gate)


def _ffn_kernel(*refs, tm, rb, on_grid, final_norm, hc):
    if final_norm:
        (top_ref, mid_ref, bot_ref, g_ref, sh_ref, sc_ref, wup_ref, cw_ref, cb_ref, wd_ref,
         gate_ref, fn_ref, o_ref, yb_ref, win_ref, val_ref, act_ref) = refs
    else:
        (top_ref, mid_ref, bot_ref, g_ref, sh_ref, sc_ref, wup_ref, cw_ref, cb_ref, wd_ref,
         gate_ref, o_ref, yb_ref, win_ref, val_ref, act_ref) = refs
        fn_ref = None
    halo = GRID_W
    guard = SUBLANES
    t = pl.program_id(1)
    nt = pl.num_programs(1)
    hid = wd_ref.shape[0]

    def modulated(v):
        v = v.astype(F32)
        y = (v * lax.rsqrt(jnp.mean(v * v, axis=-1, keepdims=True) + EPS)) * g_ref[...]
        return (y * (1.0 + sc_ref[0]) + sh_ref[0]).astype(BF16)

    yb_ref[0:halo, :] = modulated(top_ref[0])
    yb_ref[halo:halo + tm, :] = modulated(mid_ref[0])
    yb_ref[halo + tm:2 * halo + tm, :] = modulated(bot_ref[0])
    top_ok = t > 0
    bot_ok = t < nt - 1

    base = guard + halo
    for buf in range(2):
        win_ref[buf, 0:guard, :] = jnp.zeros((guard, hc), F32)
        win_ref[buf, base + tm + halo:base + tm + halo + guard, :] = jnp.zeros((guard, hc), F32)
    col = lax.broadcasted_iota(jnp.int32, (rb, hc), 0) % GRID_W
    row_taps = (-1, 0, 1) if on_grid else (0,)
    ext_rows = rb + 2 * guard
    n_rb = tm // rb
    n_chunks = hid // hc

    def up_piece(j, r):
        buf = j % 2
        c0 = j * hc
        m0 = halo + r * rb
        val_ref[buf, r * rb:(r + 1) * rb, :] = jnp.dot(
            yb_ref[m0:m0 + rb, :], wup_ref[:, c0:c0 + hc], preferred_element_type=F32)
        lo = 0 if r == 0 else m0
        hi = 2 * halo + tm if r == n_rb - 1 else m0 + rb
        gx = jnp.dot(yb_ref[lo:hi, :], wup_ref[:, hid + c0:hid + c0 + hc],
                     preferred_element_type=F32)
        if r == 0:
            win_ref[buf, guard:base, :] = jnp.where(top_ok, gx[0:halo], 0.0)
            gx = gx[halo:]
        if r == n_rb - 1:
            win_ref[buf, base + tm:base + tm + halo, :] = jnp.where(bot_ok, gx[rb:], 0.0)
            gx = gx[:rb]
        win_ref[buf, guard + m0:guard + m0 + rb, :] = gx

    def conv_piece(j, r):
        buf = j % 2
        c0 = j * hc
        parts = []
        for dc in (-1, 0, 1):
            acc = None
            for dr in row_taps:
                start = base + r * rb + dr * GRID_W - guard
                w = cw_ref[(dr + 1) * FFN_CONV + (dc + 1), :, c0:c0 + hc]
                term = w * win_ref[buf, start:start + ext_rows, :]
                acc = term if acc is None else acc + term
            parts.append(_rows_from(acc, guard + dc, rb))
        if on_grid:
            conv = (parts[1] + jnp.where(col >= 1, parts[0], 0.0)
                    + jnp.where(col <= GRID_W - 2, parts[2], 0.0))
        else:
            conv = parts[0] + parts[1] + parts[2]
        val = val_ref[buf, r * rb:(r + 1) * rb, :]
        act_ref[r * rb:(r + 1) * rb, c0:c0 + hc] = (
            _silu(conv + cb_ref[:, c0:c0 + hc]) * val).astype(BF16)

    for r in range(n_rb):
        up_piece(0, r)
    split = (n_chunks // 2) * hc
    down = None
    for j in range(n_chunks):
        for r in range(n_rb):
            if j + 1 < n_chunks:
                up_piece(j + 1, r)
            conv_piece(j, r)
        if (j + 1) * hc == split:
            down = jnp.dot(act_ref[:, 0:split], wd_ref[0:split, :], preferred_element_type=F32)
    down = down + jnp.dot(act_ref[:, split:], wd_ref[split:, :], preferred_element_type=F32)
    hn = mid_ref[0].astype(F32) + gate_ref[0] * down
    if final_norm:
        hn = (hn * lax.rsqrt(jnp.mean(hn * hn, axis=-1, keepdims=True) + EPS)) * fn_ref[...]
    o_ref[0] = hn


def _conv_ffn(hin, g, shift, scale, w_up, conv_w, conv_b, w_down, gate, *, on_grid,
              final_norm_w=None, tm=512, rb=128, hc=256, name="conv_ffn"):
    bsz, t, d = hin.shape
    hid = w_down.shape[0]
    tm = min(tm, t)
    halo = GRID_W
    nh = tm // halo
    last_h = t // halo - 1
    row = lambda b, i: (b, i, 0)
    per_batch = lambda b, i: (b, 0, 0)
    args = [hin, hin, hin, g.reshape(1, d), shift, scale, w_up,
            conv_w.reshape(FFN_CONV * FFN_CONV, 1, hid), conv_b.reshape(1, hid), w_down, gate]
    specs = [
        pl.BlockSpec((1, halo, d), lambda b, i: (b, jnp.maximum(i * nh - 1, 0), 0)),
        pl.BlockSpec((1, tm, d), row),
        pl.BlockSpec((1, halo, d), lambda b, i: (b, jnp.minimum((i + 1) * nh, last_h), 0)),
        _resident((1, d)),
        pl.BlockSpec((1, 1, d), per_batch), pl.BlockSpec((1, 1, d), per_batch),
        _resident((d, 2 * hid)),
        _resident((FFN_CONV * FFN_CONV, 1, hid)), _resident((1, hid)), _resident((hid, d)),
        pl.BlockSpec((1, 1, d), per_batch),
    ]
    if final_norm_w is not None:
        args.append(final_norm_w.reshape(1, d))
        specs.append(_resident((1, d)))
    kern = functools.partial(_ffn_kernel, tm=tm, on_grid=on_grid,
                             final_norm=final_norm_w is not None, hc=hc, rb=rb)
    return pl.pallas_call(
        kern,
        grid=(bsz, t // tm),
        in_specs=specs,
        out_specs=pl.BlockSpec((1, tm, d), row),
        out_shape=jax.ShapeDtypeStruct((bsz, t, d), F32),
        scratch_shapes=[pltpu.VMEM((tm + 2 * halo, d), BF16),
                        pltpu.VMEM((2, tm + 2 * halo + 2 * SUBLANES, hc), F32),
                        pltpu.VMEM((2, tm, hc), F32), pltpu.VMEM((tm, hid), BF16)],
        compiler_params=_params("parallel", "parallel"),
        name=name,
    )(*args)


def _group_dir_head(v):
    lead = v.shape[:-1]
    v = v.reshape(lead + (2, SSD_GROUPS, SSD_HPG))
    return jnp.swapaxes(v, -3, -2).reshape(lead + (2 * SSD_GROUPS * SSD_HPG,))


def kernel(x, c, ctx, c_ctx, mod_w, mod_b, norm1_w, norm2_w, ssd_w_in, ssd_conv_w, ssd_conv_b, ssd_dt_bias, ssd_a_log, ssd_d, ssd_norm_w, ssd_w_out, conf_w_pw1, conf_b_pw1, conf_w_dw, conf_b_dw, conf_ln_w, conf_ln_b, conf_w_pw2, conf_b_pw2, ffn_w_up, ffn_conv_w, ffn_conv_b, ffn_w_down, final_norm_w):
    bsz, seq, d = x.shape
    depth = mod_w.shape[0]
    g, h, hd, n = SSD_GROUPS, SSD_HPG, SSD_HEAD_DIM, SSD_STATE
    n_heads2 = 2 * g * h

    cond = jnp.concatenate([c, c_ctx[None, :], jnp.zeros((16 - bsz - 1, d), F32)], axis=0)
    mods = _ada_params(cond, mod_w, mod_b).reshape(depth, 16, 6, d)

    def latent_mod(i, k):
        return mods[i, :bsz, k][:, None, :]

    def context_mod(i, k):
        return jnp.broadcast_to(mods[i, bsz, k][None, None, :], (bsz, 1, d))

    hl, hc = x, ctx
    for i in range(depth):
        kind = i % 2
        j = i // 2
        last = i == depth - 1
        need_ctx = (not last) or kind == 0
        sh1, sc1, g1, sh2, sc2, g2 = (latent_mod(i, k) for k in range(6))
        if need_ctx:
            csh1, csc1, cg1, csh2, csc2, cg2 = (context_mod(i, k) for k in range(6))
        if kind == 0:
            w_in = ssd_w_in[j]
            dt0 = w_in.shape[1] - n_heads2
            w_in = jnp.concatenate(
                [w_in[:, :dt0], _group_dir_head(w_in[:, dt0:]),
                 jnp.zeros((d, LANES - n_heads2), F32)], axis=1).astype(BF16)
            lane_pad = lambda v: jnp.pad(_group_dir_head(v.reshape(-1)),
                                         (0, LANES - n_heads2)).reshape(1, LANES)
            dt_bias, a_log = lane_pad(ssd_dt_bias[j]), lane_pad(ssd_a_log[j])
            expand = (jnp.arange(LANES)[:, None] == jnp.arange(n_heads2 * hd)[None, :] // hd
                      ).astype(BF16)
            conv_b = ssd_conv_b[j].reshape(1, -1)
            d_lanes = jnp.repeat(ssd_d[j], hd).reshape(1, g * h * hd)
            w_out = ssd_w_out[j].astype(BF16)

            def mixer(hin, shift, scale, s0, name):
                proj = _norm_matmul(hin, norm1_w[i], w_in, shift=shift, scale=scale, name=name)
                dtf = _dt_factors(proj, dt_bias, a_log, expand)
                return _ssd_scan(proj, dtf, ssd_conv_w[j], conv_b, d_lanes, s0)

            zeros = jnp.zeros((bsz, g, 2, n, h * hd), F32)
            yc, s_ctx = mixer(hc, csh1, csc1, zeros, "ssd_in_ctx")
            yl, _ = mixer(hl, sh1, sc1, s_ctx, "ssd_in")
            hl = _norm_matmul(yl, ssd_norm_w[j], w_out, post="resid", res=hl, gate=g1,
                              name="ssd_out")
            if not last:
                hc = _norm_matmul(yc, ssd_norm_w[j], w_out, post="resid", res=hc, gate=cg1,
                                  name="ssd_out_ctx")
        else:
            w1 = conf_w_pw1[j].astype(BF16)
            glu = _norm_matmul(hl, norm1_w[i], w1, shift=sh1, scale=sc1, bias=conf_b_pw1[j],
                               post="glu", name="conf_pw1")
            hl_new = _conformer_tail(glu, conf_w_dw[j], conf_b_dw[j], conf_ln_w[j], conf_ln_b[j],
                                     conf_w_pw2[j].astype(BF16), conf_b_pw2[j], hl, g1)
            if not last:
                gluc = _norm_matmul(hc, norm1_w[i], w1, shift=csh1, scale=csc1,
                                    bias=conf_b_pw1[j], post="glu", name="conf_pw1_ctx")
                hc = _conformer_tail(gluc, conf_w_dw[j], conf_b_dw[j], conf_ln_w[j], conf_ln_b[j],
                                     conf_w_pw2[j].astype(BF16), conf_b_pw2[j], hc, cg1)
            hl = hl_new
        w_up = ffn_w_up[i].astype(BF16)
        w_down = ffn_w_down[i].astype(BF16)
        hl = _conv_ffn(hl, norm2_w[i], sh2, sc2, w_up, ffn_conv_w[i], ffn_conv_b[i], w_down, g2,
                       on_grid=True, final_norm_w=final_norm_w if last else None)
        if not last:
            hc = _conv_ffn(hc, norm2_w[i], csh2, csc2, w_up, ffn_conv_w[i], ffn_conv_b[i], w_down,
                           cg2, on_grid=False, name="conv_ffn_ctx")
    return hl
```

```python
import functools

import jax
import jax.numpy as jnp
from jax import lax
from jax.experimental import pallas as pl
from jax.experimental.pallas import tpu as pltpu

F32 = jnp.float32
BF16 = jnp.bfloat16

EPS = 1e-6
LANES = 128
SUBLANES = 8
GRID_W = 64
SSD_GROUPS = 8
SSD_HPG = 4
SSD_HEAD_DIM = 64
SSD_STATE = 128
SSD_CHUNK = 128
SSD_CONV = 5
CONF_KERNEL = 31
FFN_CONV = 3

VMEM_LIMIT_BYTES = 56 * 1024 * 1024


def _params(*semantics):
    return pltpu.CompilerParams(dimension_semantics=semantics,
                                vmem_limit_bytes=VMEM_LIMIT_BYTES)


def _sigmoid(v):
    return 0.5 * jnp.tanh(0.5 * v) + 0.5


def _silu(v):
    return v * _sigmoid(v)


def _softplus(v):
    return jnp.maximum(v, 0.0) + jnp.log1p(jnp.exp(-jnp.abs(v)))


def _split3(v):
    v1 = v.astype(BF16)
    r = v - v1.astype(F32)
    v2 = r.astype(BF16)
    r = r - v2.astype(F32)
    return v1, v2, r.astype(BF16)


def _rows_from(ext, start, n):
    total = ext.shape[0]
    if start % SUBLANES == 0:
        return ext[start:start + n]
    lo = (start // SUBLANES) * SUBLANES
    rolled = pltpu.roll(ext, (lo - start) % total, 0)
    return rolled[lo:lo + n]


def _resident(shape):
    nd = len(shape)
    return pl.BlockSpec(shape, lambda *_: (0,) * nd, pipeline_mode=pl.Buffered(1))


def _ada_kernel(cond_ref, w_ref, b_ref, o_ref):
    a = _silu(cond_ref[...]).astype(BF16)
    o_ref[0] = jnp.dot(a, w_ref[0].astype(BF16), preferred_element_type=F32) + b_ref[0]


def _ada_params(cond, mod_w, mod_b, tn=1536):
    depth, d, n = mod_w.shape
    rows = cond.shape[0]
    return pl.pallas_call(
        _ada_kernel,
        grid=(depth, n // tn),
        in_specs=[pl.BlockSpec((rows, d), lambda i, j: (0, 0)),
                  pl.BlockSpec((1, d, tn), lambda i, j: (i, 0, j)),
                  pl.BlockSpec((1, 1, tn), lambda i, j: (i, 0, j))],
        out_specs=pl.BlockSpec((1, rows, tn), lambda i, j: (i, 0, j)),
        out_shape=jax.ShapeDtypeStruct((depth, rows, n), F32),
        compiler_params=_params("parallel", "parallel"),
        name="ada_params",
    )(cond, mod_w, mod_b.reshape(depth, 1, n))


def _mm_kernel(*refs, modulate, post, has_bias, n_out, n_chunk):
    it = iter(refs)
    x_ref = next(it)
    g_ref = next(it)
    sh_ref = sc_ref = b_ref = res_ref = gate_ref = None
    if modulate:
        sh_ref, sc_ref = next(it), next(it)
    w_ref = next(it)
    if has_bias:
        b_ref = next(it)
    if post == "resid":
        res_ref, gate_ref = next(it), next(it)
    o_ref = next(it)

    x = x_ref[0].astype(F32)
    ms = jnp.mean(x * x, axis=-1, keepdims=True)
    y = (x * lax.rsqrt(ms + EPS)) * g_ref[...]
    if modulate:
        y = y * (1.0 + sc_ref[0]) + sh_ref[0]
    yb = y.astype(BF16)

    def proj(c0):
        acc = jnp.dot(yb, w_ref[:, c0:c0 + n_chunk], preferred_element_type=F32)
        if has_bias:
            acc = acc + b_ref[:, c0:c0 + n_chunk]
        return acc

    for n0 in range(0, n_out, n_chunk):
        if post == "glu":
            out = proj(n0) * _sigmoid(proj(n_out + n0))
        elif post == "resid":
            out = res_ref[0, :, n0:n0 + n_chunk] + gate_ref[0, :, n0:n0 + n_chunk] * proj(n0)
        else:
            out = proj(n0)
        o_ref[0, :, n0:n0 + n_chunk] = out.astype(o_ref.dtype)


def _norm_matmul(x, g, w, *, shift=None, scale=None, bias=None, post="store",
                 res=None, gate=None, out_dtype=F32, tm=512, n_chunk=1024, name=None):
    bsz, t, k = x.shape
    nw = w.shape[1]
    n_out = nw // 2 if post == "glu" else nw
    tm = min(tm, t)
    n_chunk = max(c for c in range(LANES, n_chunk + 1, LANES) if n_out % c == 0)
    assert t % tm == 0
    modulate = shift is not None
    row = lambda b, i: (b, i, 0)
    per_batch = lambda b, i: (b, 0, 0)
    args = [x, g.reshape(1, k)]
    specs = [pl.BlockSpec((1, tm, k), row), _resident((1, k))]
    if modulate:
        args += [shift, scale]
        specs += [pl.BlockSpec((1, 1, k), per_batch)] * 2
    args.append(w)
    specs.append(_resident((k, nw)))
    if bias is not None:
        args.append(bias.reshape(1, nw))
        specs.append(_resident((1, nw)))
    if post == "resid":
        args += [res, gate]
        specs += [pl.BlockSpec((1, tm, n_out), row), pl.BlockSpec((1, 1, n_out), per_batch)]
    kern = functools.partial(_mm_kernel, modulate=modulate, post=post,
                             has_bias=bias is not None, n_out=n_out, n_chunk=n_chunk)
    return pl.pallas_call(
        kern,
        grid=(bsz, t // tm),
        in_specs=specs,
        out_specs=pl.BlockSpec((1, tm, n_out), row),
        out_shape=jax.ShapeDtypeStruct((bsz, t, n_out), out_dtype),
        compiler_params=_params("parallel", "parallel"),
        name=name,
    )(*args)


def _dt_kernel(raw_ref, bias_ref, alog_ref, e_ref,
               cumc_ref, wc_ref, eac_ref, dtr_ref, cumr_ref, dec_ref, tot_ref, *, seq):
    q = SSD_CHUNK
    nc = seq // q
    ii = lax.broadcasted_iota(jnp.int32, (q, q), 0)
    jj = lax.broadcasted_iota(jnp.int32, (q, q), 1)
    lower_b = jnp.where(jj <= ii, 1.0, 0.0).astype(BF16)
    upper_b = jnp.where(jj >= ii, 1.0, 0.0).astype(BF16)
    lane = lax.broadcasted_iota(jnp.int32, (1, LANES), 1)
    backward = (lane // SSD_HPG) % 2 == 1
    bias = bias_ref[...]
    a_neg = -jnp.exp(alog_ref[...])

    def body(ci, carry):
        rows = pl.ds(pl.multiple_of(ci * q, q), q)
        dt = _softplus(raw_ref[0, rows, :].astype(F32) + bias)
        parts = _split3(dt * a_neg)
        cum_f = sum(jnp.dot(lower_b, p, preferred_element_type=F32) for p in parts)
        cum_b = sum(jnp.dot(upper_b, p, preferred_element_type=F32) for p in parts)
        cum = jnp.where(backward, cum_b, cum_f)
        tot = cum_f[q - 1:q, :]
        cumc_ref[0, rows, :] = cum
        wc_ref[0, rows, :] = (dt * jnp.exp(tot - cum)).astype(BF16)
        eac_ref[0, rows, :] = jnp.exp(cum).astype(BF16)
        dtr_ref[0, :, rows] = dt.T
        cumr_ref[0, :, rows] = cum.T
        tot_ref[pl.ds(ci, 1), :] = tot
        return carry

    lax.fori_loop(0, nc, body, 0)
    dec = jnp.exp(tot_ref[...])
    dec_ref[0] = sum(jnp.dot(p, e_ref[...], preferred_element_type=F32) for p in _split3(dec))


def _dt_factors(proj, bias, alog, expand):
    bsz, seq, width = proj.shape
    nc = seq // SSD_CHUNK
    blk = width // LANES - 1
    n_exp = expand.shape[1]
    col = pl.BlockSpec((1, seq, LANES), lambda b: (b, 0, 0))
    rowb = pl.BlockSpec((1, LANES, seq), lambda b: (b, 0, 0))
    return pl.pallas_call(
        functools.partial(_dt_kernel, seq=seq),
        grid=(bsz,),
        in_specs=[pl.BlockSpec((1, seq, LANES), lambda b: (b, 0, blk)),
                  _resident((1, LANES)), _resident((1, LANES)), _resident((LANES, n_exp))],
        out_specs=[col, col, col, rowb, rowb,
                   pl.BlockSpec((1, nc, n_exp), lambda b: (b, 0, 0))],
        out_shape=[jax.ShapeDtypeStruct((bsz, seq, LANES), F32),
                   jax.ShapeDtypeStruct((bsz, seq, LANES), BF16),
                   jax.ShapeDtypeStruct((bsz, seq, LANES), BF16),
                   jax.ShapeDtypeStruct((bsz, LANES, seq), F32),
                   jax.ShapeDtypeStruct((bsz, LANES, seq), F32),
                   jax.ShapeDtypeStruct((bsz, nc, n_exp), F32)],
        scratch_shapes=[pltpu.VMEM((nc, LANES), F32)],
        compiler_params=_params("parallel"),
        name="ssd_dt",
    )(proj, bias, alog, expand)


def _ssd_kernel(x_ref, b_ref, c_ref, z_ref, cumc_ref, wc_ref, eac_ref, dtr_ref, cumr_ref,
                dec_ref, d_ref, s0_ref,
                y_ref, sn_ref,
                sc_ref, sent_ref, ccol_ref, st_ref, e_ref, *, seq):
    q = SSD_CHUNK
    nc = seq // q
    nh, hd = SSD_HPG, SSD_HEAD_DIM
    xw = nh * hd
    g = pl.program_id(1)

    kk = lax.broadcasted_iota(jnp.int32, (LANES, xw), 0)
    head_of_lane = lax.broadcasted_iota(jnp.int32, (LANES, xw), 1) // hd
    for d in range(2):
        e_ref[d] = jnp.where(kk == g * 2 * nh + d * nh + head_of_lane, 1.0, 0.0).astype(BF16)

    for gg in range(SSD_GROUPS):
        @pl.when(g == gg)
        def _():
            def cp_body(ci, carry):
                rows = pl.ds(pl.multiple_of(ci * q, q), q)
                ccol_ref[rows, :] = cumc_ref[0, rows, gg * 2 * nh:(gg + 1) * 2 * nh]
                return carry
            lax.fori_loop(0, nc, cp_body, 0, unroll=True)

    ii = lax.broadcasted_iota(jnp.int32, (q, q), 0)
    jj = lax.broadcasted_iota(jnp.int32, (q, q), 1)
    lane_head = lax.broadcasted_iota(jnp.int32, (q, xw), 1) // hd

    def intra(ci, carry):
        rows = pl.ds(pl.multiple_of(ci * q, q), q)
        xc = x_ref[0, rows, :]
        x_b = xc.astype(BF16)
        btb = b_ref[0, rows, :].astype(F32).T.astype(BF16)
        cbf = c_ref[0, rows, :]
        cb = jnp.dot(cbf, btb, preferred_element_type=F32)
        cum8 = ccol_ref[rows, :]
        lm = []
        for h in range(nh):
            acf, acb = cum8[:, h:h + 1], cum8[:, nh + h:nh + h + 1]
            arf = cumr_ref[0, h:h + 1, rows]
            arb = cumr_ref[0, nh + h:nh + h + 1, rows]
            dtf = dtr_ref[0, h:h + 1, rows]
            dtb = dtr_ref[0, nh + h:nh + h + 1, rows]
            arg = jnp.where(jj <= ii, acf - arf, acb - arb)
            dts = jnp.where(jj < ii, dtf, jnp.where(jj > ii, dtb, dtf + dtb))
            lm.append((jnp.exp(arg) * (cb * dts)).astype(BF16))
        masked = [jnp.where(lane_head == h, x_b, jnp.zeros_like(x_b)) for h in range(nh)]
        yd = (jnp.dot(jnp.concatenate(lm[0:2], axis=1), jnp.concatenate(masked[0:2], axis=0),
                      preferred_element_type=F32)
              + jnp.dot(jnp.concatenate(lm[2:4], axis=1), jnp.concatenate(masked[2:4], axis=0),
                        preferred_element_type=F32))
        y_ref[0, rows, :] = d_ref[...] * xc + yd
        for d in range(2):
            wexp = jnp.dot(wc_ref[0, rows, :], e_ref[d], preferred_element_type=F32)
            sc_ref[ci, d] = jnp.dot(btb, (xc * wexp).astype(BF16), preferred_element_type=F32)
        return carry

    lax.fori_loop(0, nc, intra, 0, unroll=min(4, nc))

    st_ref[...] = s0_ref[0, 0]

    def carry_states(c, carry):
        for d in range(2):
            ci = c if d == 0 else nc - 1 - c
            s_in = st_ref[d]
            sent_ref[ci, d] = s_in.astype(BF16)
            dec = dec_ref[0, pl.ds(ci, 1), d * xw:(d + 1) * xw]
            st_ref[d] = dec * s_in + sc_ref[ci, d]
        return carry

    lax.fori_loop(0, nc, carry_states, 0)
    sn_ref[0, 0] = st_ref[...]

    def inter(ci, carry):
        rows = pl.ds(pl.multiple_of(ci * q, q), q)
        cbf = c_ref[0, rows, :]
        y = y_ref[0, rows, :]
        for d in range(2):
            ea = jnp.dot(eac_ref[0, rows, :], e_ref[d], preferred_element_type=F32)
            y = y + ea * jnp.dot(cbf, sent_ref[ci, d], preferred_element_type=F32)
        y_ref[0, rows, :] = (y * _silu(z_ref[0, rows, :].astype(F32))).astype(y_ref.dtype)
        return carry

    lax.fori_loop(0, nc, inter, 0, unroll=min(4, nc))


def _ssd_scan(z, xconv, bc, dtf, d_lanes, s0):
    cumc, wc, eac, dtr, cumr, dec = dtf
    bsz, seq, di = z.shape
    g, h, hd, n = SSD_GROUPS, SSD_HPG, SSD_HEAD_DIM, SSD_STATE
    q = SSD_CHUNK
    nc = seq // q
    xw = h * hd
    state_spec = pl.BlockSpec((1, 1, 2, n, xw), lambda b, i: (b, i, 0, 0, 0))
    chan = pl.BlockSpec((1, seq, xw), lambda b, i: (b, 0, i))
    col = pl.BlockSpec((1, seq, LANES), lambda b, i: (b, 0, 0))
    rowb = pl.BlockSpec((1, 2 * h, seq), lambda b, i: (b, i, 0))
    kern = functools.partial(_ssd_kernel, seq=seq)
    return pl.pallas_call(
        kern,
        grid=(bsz, g),
        in_specs=[
            chan,
            pl.BlockSpec((1, seq, n), lambda b, i: (b, 0, i)),
            pl.BlockSpec((1, seq, n), lambda b, i: (b, 0, g + i)),
            chan,
            col, col, col, rowb, rowb,
            pl.BlockSpec((1, nc, 2 * xw), lambda b, i: (b, 0, i)),
            pl.BlockSpec((1, xw), lambda b, i: (0, i)),
            state_spec,
        ],
        out_specs=[chan, state_spec],
        out_shape=[jax.ShapeDtypeStruct((bsz, seq, di), F32),
                   jax.ShapeDtypeStruct((bsz, g, 2, n, xw), F32)],
        scratch_shapes=[pltpu.VMEM((nc, 2, n, xw), F32),
                        pltpu.VMEM((nc, 2, n, xw), BF16), pltpu.VMEM((seq, 2 * h), F32),
                        pltpu.VMEM((2, n, xw), F32), pltpu.VMEM((2, LANES, xw), BF16)],
        compiler_params=_params("parallel", "parallel"),
        name="ssd_scan",
    )(xconv, bc, bc, z, cumc, wc, eac, dtr, cumr, dec, d_lanes, s0)


def _ssd_in_kernel(top_ref, mid_ref, bot_ref, g_ref, sh_ref, sc_ref, w_ref, cw_ref, cb_ref,
                   z_ref, x_ref, bc_ref, dt_ref, yb_ref, win_ref, *, tm, rb, cw, nz):
    halo = SUBLANES
    t = pl.program_id(1)
    nt = pl.num_programs(1)
    di = z_ref.shape[2]

    def modulated(v):
        v = v.astype(F32)
        y = (v * lax.rsqrt(jnp.mean(v * v, axis=-1, keepdims=True) + EPS)) * g_ref[...]
        return (y * (1.0 + sc_ref[0]) + sh_ref[0]).astype(BF16)

    yb_ref[0:halo, :] = modulated(top_ref[0])
    yb_ref[halo:halo + tm, :] = modulated(mid_ref[0])
    yb_ref[halo + tm:2 * halo + tm, :] = modulated(bot_ref[0])
    top_ok = t > 0
    bot_ok = t < nt - 1

    dt0 = w_ref.shape[1] - LANES
    dt_ref[0] = jnp.dot(yb_ref[halo:halo + tm, :], w_ref[:, dt0:], preferred_element_type=F32)

    n_conv = 2 * di
    z_every = (n_conv // cw) // (di // nz)
    for ci, c0 in enumerate(range(0, n_conv, cw)):
        buf = ci % 2
        if ci % z_every == 0:
            n0 = (ci // z_every) * nz
            z_ref[0, :, n0:n0 + nz] = jnp.dot(yb_ref[halo:halo + tm, :], w_ref[:, n0:n0 + nz],
                                              preferred_element_type=F32).astype(z_ref.dtype)
        p = jnp.dot(yb_ref[...], w_ref[:, di + c0:di + c0 + cw], preferred_element_type=F32)
        win_ref[buf, 0:halo, :] = jnp.where(top_ok, p[0:halo], 0.0)
        win_ref[buf, halo:halo + tm, :] = p[halo:halo + tm]
        win_ref[buf, halo + tm:2 * halo + tm, :] = jnp.where(bot_ok, p[halo + tm:], 0.0)
        w = cw_ref[:, c0:c0 + cw]
        bias = cb_ref[:, c0:c0 + cw]
        for r0 in range(0, tm, rb):
            ext = win_ref[buf, r0:r0 + rb + 2 * halo, :]
            acc = bias + w[SSD_CONV // 2:SSD_CONV // 2 + 1, :] * ext[halo:halo + rb]
            for k in range(SSD_CONV):
                if k != SSD_CONV // 2:
                    acc = acc + w[k:k + 1, :] * _rows_from(ext, halo - SSD_CONV // 2 + k, rb)
            out = _silu(acc)
            if c0 < di:
                x_ref[0, r0:r0 + rb, c0:c0 + cw] = out
            else:
                bc_ref[0, r0:r0 + rb, c0 - di:c0 - di + cw] = out.astype(bc_ref.dtype)


def _ssd_in_proj(hin, g, shift, scale, w_in, conv_w, conv_b, *, tm=512, rb=128, cw=256, nz=512,
                 name="ssd_in"):
    bsz, t, d = hin.shape
    nw = w_in.shape[1]
    di = (nw - LANES) // 3
    tm = min(tm, t)
    halo = SUBLANES
    nh = tm // halo
    last_h = t // halo - 1
    row = lambda b, i: (b, i, 0)
    per_batch = lambda b, i: (b, 0, 0)
    kern = functools.partial(_ssd_in_kernel, tm=tm, rb=rb, cw=cw, nz=nz)
    return pl.pallas_call(
        kern,
        grid=(bsz, t // tm),
        in_specs=[
            pl.BlockSpec((1, halo, d), lambda b, i: (b, jnp.maximum(i * nh - 1, 0), 0)),
            pl.BlockSpec((1, tm, d), row),
            pl.BlockSpec((1, halo, d), lambda b, i: (b, jnp.minimum((i + 1) * nh, last_h), 0)),
            _resident((1, d)),
            pl.BlockSpec((1, 1, d), per_batch), pl.BlockSpec((1, 1, d), per_batch),
            _resident((d, nw)), _resident((SSD_CONV, 2 * di)), _resident((1, 2 * di)),
        ],
        out_specs=[pl.BlockSpec((1, tm, di), row), pl.BlockSpec((1, tm, di), row),
                   pl.BlockSpec((1, tm, di), row), pl.BlockSpec((1, tm, LANES), row)],
        out_shape=[jax.ShapeDtypeStruct((bsz, t, di), F32),
                   jax.ShapeDtypeStruct((bsz, t, di), F32),
                   jax.ShapeDtypeStruct((bsz, t, di), BF16),
                   jax.ShapeDtypeStruct((bsz, t, LANES), F32)],
        scratch_shapes=[pltpu.VMEM((tm + 2 * halo, d), BF16),
                        pltpu.VMEM((2, tm + 2 * halo, cw), F32)],
        compiler_params=_params("parallel", "parallel"),
        name=name,
    )(hin, hin, hin, g.reshape(1, d), shift, scale, w_in, conv_w, conv_b)


def _conf_kernel(top_ref, mid_ref, bot_ref, wdw_ref, bdw_ref, lnw_ref, lnb_ref,
                 w2_ref, b2_ref, res_ref, gate_ref, o_ref, win_ref, cv_ref, *, tm, halo):
    t = pl.program_id(1)
    nt = pl.num_programs(1)
    win_ref[0:halo, :] = jnp.where(t > 0, top_ref[0], 0.0).astype(F32)
    win_ref[halo:halo + tm, :] = mid_ref[0].astype(F32)
    win_ref[halo + tm:2 * halo + tm, :] = jnp.where(t < nt - 1, bot_ref[0], 0.0).astype(F32)
    ch = mid_ref.shape[2]
    pad = CONF_KERNEL // 2
    ext_rows = tm + SUBLANES
    for c0 in range(0, ch, LANES):
        cs = slice(c0, c0 + LANES)
        acc = jnp.zeros((tm, LANES), F32) + bdw_ref[:, cs]
        for r in range(SUBLANES):
            part = None
            for a in range(-2, 2):
                k = SUBLANES * a + r + pad
                if 0 <= k < CONF_KERNEL:
                    start = halo + SUBLANES * a
                    term = wdw_ref[k:k + 1, cs] * win_ref[start:start + ext_rows, cs]
                    part = term if part is None else part + term
            acc = acc + _rows_from(part, r, tm)
        cv_ref[:, cs] = acc
    hcv = cv_ref[...]
    mu = jnp.mean(hcv, axis=-1, keepdims=True)
    dv = hcv - mu
    yn = dv * lax.rsqrt(jnp.mean(dv * dv, axis=-1, keepdims=True) + EPS)
    yn = _silu(yn * lnw_ref[...] + lnb_ref[...])
    out = jnp.dot(yn.astype(BF16), w2_ref[...], preferred_element_type=F32) + b2_ref[...]
    o_ref[0] = res_ref[0] + gate_ref[0] * out


def _conformer_tail(glu, w_dw, b_dw, ln_w, ln_b, w2, b2, res, gate, tm=128, halo=16):
    bsz, t, ch = glu.shape
    nh = tm // halo
    last_h = t // halo - 1
    row = lambda b, i: (b, i, 0)
    kern = functools.partial(_conf_kernel, tm=tm, halo=halo)
    return pl.pallas_call(
        kern,
        grid=(bsz, t // tm),
        in_specs=[
            pl.BlockSpec((1, halo, ch), lambda b, i: (b, jnp.maximum(i * nh - 1, 0), 0)),
            pl.BlockSpec((1, tm, ch), row),
            pl.BlockSpec((1, halo, ch), lambda b, i: (b, jnp.minimum((i + 1) * nh, last_h), 0)),
            _resident((CONF_KERNEL, ch)), _resident((1, ch)), _resident((1, ch)),
            _resident((1, ch)), _resident((ch, ch)), _resident((1, ch)),
            pl.BlockSpec((1, tm, ch), row),
            pl.BlockSpec((1, 1, ch), lambda b, i: (b, 0, 0)),
        ],
        out_specs=pl.BlockSpec((1, tm, ch), row),
        out_shape=jax.ShapeDtypeStruct((bsz, t, ch), F32),
        scratch_shapes=[pltpu.VMEM((tm + 2 * halo, ch), F32), pltpu.VMEM((tm, ch), F32)],
        compiler_params=_params("parallel", "parallel"),
        name="conformer_tail",
    )(glu, glu, glu, w_dw, b_dw.reshape(1, ch), ln_w.reshape(1, ch), ln_b.reshape(1, ch),
      w2, b2.reshape(1, ch), res, gate)


def _ffn_kernel(*refs, tm, rb, on_grid, final_norm, hc):
    if final_norm:
        (top_ref, mid_ref, bot_ref, g_ref, sh_ref, sc_ref, wup_ref, cw_ref, cb_ref, wd_ref,
         gate_ref, fn_ref, o_ref, yb_ref, win_ref, val_ref, act_ref) = refs
    else:
        (top_ref, mid_ref, bot_ref, g_ref, sh_ref, sc_ref, wup_ref, cw_ref, cb_ref, wd_ref,
         gate_ref, o_ref, yb_ref, win_ref, val_ref, act_ref) = refs
        fn_ref = None
    halo = GRID_W
    guard = SUBLANES
    t = pl.program_id(1)
    nt = pl.num_programs(1)
    hid = wd_ref.shape[0]

    def modulated(v):
        v = v.astype(F32)
        y = (v * lax.rsqrt(jnp.mean(v * v, axis=-1, keepdims=True) + EPS)) * g_ref[...]
        return (y * (1.0 + sc_ref[0]) + sh_ref[0]).astype(BF16)

    yb_ref[0:halo, :] = modulated(top_ref[0])
    yb_ref[halo:halo + tm, :] = modulated(mid_ref[0])
    yb_ref[halo + tm:2 * halo + tm, :] = modulated(bot_ref[0])
    top_ok = t > 0
    bot_ok = t < nt - 1

    base = guard + halo
    for buf in range(2):
        win_ref[buf, 0:guard, :] = jnp.zeros((guard, hc), F32)
        win_ref[buf, base + tm + halo:base + tm + halo + guard, :] = jnp.zeros((guard, hc), F32)
    col = lax.broadcasted_iota(jnp.int32, (rb, hc), 0) % GRID_W
    row_taps = (-1, 0, 1) if on_grid else (0,)
    ext_rows = rb + 2 * guard
    n_rb = tm // rb
    n_chunks = hid // hc

    def up_piece(j, r):
        buf = j % 2
        c0 = j * hc
        m0 = halo + r * rb
        val_ref[buf, r * rb:(r + 1) * rb, :] = jnp.dot(
            yb_ref[m0:m0 + rb, :], wup_ref[:, c0:c0 + hc], preferred_element_type=F32)
        lo = 0 if r == 0 else m0
        hi = 2 * halo + tm if r == n_rb - 1 else m0 + rb
        gx = jnp.dot(yb_ref[lo:hi, :], wup_ref[:, hid + c0:hid + c0 + hc],
                     preferred_element_type=F32)
        if r == 0:
            win_ref[buf, guard:base, :] = jnp.where(top_ok, gx[0:halo], 0.0)
            gx = gx[halo:]
        if r == n_rb - 1:
            win_ref[buf, base + tm:base + tm + halo, :] = jnp.where(bot_ok, gx[rb:], 0.0)
            gx = gx[:rb]
        win_ref[buf, guard + m0:guard + m0 + rb, :] = gx

    def conv_piece(j, r):
        buf = j % 2
        c0 = j * hc
        parts = []
        for dc in (-1, 0, 1):
            acc = None
            for dr in row_taps:
                start = base + r * rb + dr * GRID_W - guard
                w = cw_ref[(dr + 1) * FFN_CONV + (dc + 1), :, c0:c0 + hc]
                term = w * win_ref[buf, start:start + ext_rows, :]
                acc = term if acc is None else acc + term
            parts.append(_rows_from(acc, guard + dc, rb))
        if on_grid:
            conv = (parts[1] + jnp.where(col >= 1, parts[0], 0.0)
                    + jnp.where(col <= GRID_W - 2, parts[2], 0.0))
        else:
            conv = parts[0] + parts[1] + parts[2]
        val = val_ref[buf, r * rb:(r + 1) * rb, :]
        act_ref[r * rb:(r + 1) * rb, c0:c0 + hc] = (
            _silu(conv + cb_ref[:, c0:c0 + hc]) * val).astype(BF16)

    for r in range(n_rb):
        up_piece(0, r)
    split = (n_chunks // 2) * hc
    down = None
    for j in range(n_chunks):
        for r in range(n_rb):
            conv_piece(j, r)
            if j + 1 < n_chunks:
                up_piece(j + 1, r)
        if (j + 1) * hc == split:
            down = jnp.dot(act_ref[:, 0:split], wd_ref[0:split, :], preferred_element_type=F32)
    down = down + jnp.dot(act_ref[:, split:], wd_ref[split:, :], preferred_element_type=F32)
    hn = mid_ref[0].astype(F32) + gate_ref[0] * down
    if final_norm:
        hn = (hn * lax.rsqrt(jnp.mean(hn * hn, axis=-1, keepdims=True) + EPS)) * fn_ref[...]
    o_ref[0] = hn


def _conv_ffn(hin, g, shift, scale, w_up, conv_w, conv_b, w_down, gate, *, on_grid,
              final_norm_w=None, tm=512, rb=128, hc=256, name="conv_ffn"):
    bsz, t, d = hin.shape
    hid = w_down.shape[0]
    tm = min(tm, t)
    halo = GRID_W
    nh = tm // halo
    last_h = t // halo - 1
    row = lambda b, i: (b, i, 0)
    per_batch = lambda b, i: (b, 0, 0)
    args = [hin, hin, hin, g.reshape(1, d), shift, scale, w_up,
            conv_w.reshape(FFN_CONV * FFN_CONV, 1, hid), conv_b.reshape(1, hid), w_down, gate]
    specs = [
        pl.BlockSpec((1, halo, d), lambda b, i: (b, jnp.maximum(i * nh - 1, 0), 0)),
        pl.BlockSpec((1, tm, d), row),
        pl.BlockSpec((1, halo, d), lambda b, i: (b, jnp.minimum((i + 1) * nh, last_h), 0)),
        _resident((1, d)),
        pl.BlockSpec((1, 1, d), per_batch), pl.BlockSpec((1, 1, d), per_batch),
        _resident((d, 2 * hid)),
        _resident((FFN_CONV * FFN_CONV, 1, hid)), _resident((1, hid)), _resident((hid, d)),
        pl.BlockSpec((1, 1, d), per_batch),
    ]
    if final_norm_w is not None:
        args.append(final_norm_w.reshape(1, d))
        specs.append(_resident((1, d)))
    kern = functools.partial(_ffn_kernel, tm=tm, on_grid=on_grid,
                             final_norm=final_norm_w is not None, hc=hc, rb=rb)
    return pl.pallas_call(
        kern,
        grid=(bsz, t // tm),
        in_specs=specs,
        out_specs=pl.BlockSpec((1, tm, d), row),
        out_shape=jax.ShapeDtypeStruct((bsz, t, d), F32),
        scratch_shapes=[pltpu.VMEM((tm + 2 * halo, d), BF16),
                        pltpu.VMEM((2, tm + 2 * halo + 2 * SUBLANES, hc), F32),
                        pltpu.VMEM((2, tm, hc), F32), pltpu.VMEM((tm, hid), BF16)],
        compiler_params=_params("parallel", "parallel"),
        name=name,
    )(*args)


def _group_dir_head(v):
    lead = v.shape[:-1]
    v = v.reshape(lead + (2, SSD_GROUPS, SSD_HPG))
    return jnp.swapaxes(v, -3, -2).reshape(lead + (2 * SSD_GROUPS * SSD_HPG,))


def kernel(x, c, ctx, c_ctx, mod_w, mod_b, norm1_w, norm2_w, ssd_w_in, ssd_conv_w, ssd_conv_b, ssd_dt_bias, ssd_a_log, ssd_d, ssd_norm_w, ssd_w_out, conf_w_pw1, conf_b_pw1, conf_w_dw, conf_b_dw, conf_ln_w, conf_ln_b, conf_w_pw2, conf_b_pw2, ffn_w_up, ffn_conv_w, ffn_conv_b, ffn_w_down, final_norm_w):
    bsz, seq, d = x.shape
    depth = mod_w.shape[0]
    g, h, hd, n = SSD_GROUPS, SSD_HPG, SSD_HEAD_DIM, SSD_STATE
    n_heads2 = 2 * g * h

    cond = jnp.concatenate([c, c_ctx[None, :], jnp.zeros((16 - bsz - 1, d), F32)], axis=0)
    mods = _ada_params(cond, mod_w, mod_b).reshape(depth, 16, 6, d)

    def latent_mod(i, k):
        return mods[i, :bsz, k][:, None, :]

    def context_mod(i, k):
        return jnp.broadcast_to(mods[i, bsz, k][None, None, :], (bsz, 1, d))

    hl, hc = x, ctx
    for i in range(depth):
        kind = i % 2
        j = i // 2
        last = i == depth - 1
        need_ctx = (not last) or kind == 0
        sh1, sc1, g1, sh2, sc2, g2 = (latent_mod(i, k) for k in range(6))
        if need_ctx:
            csh1, csc1, cg1, csh2, csc2, cg2 = (context_mod(i, k) for k in range(6))
        if kind == 0:
            w_in = ssd_w_in[j]
            dt0 = w_in.shape[1] - n_heads2
            w_in = jnp.concatenate(
                [w_in[:, :dt0].astype(BF16), _group_dir_head(w_in[:, dt0:]).astype(BF16),
                 jnp.zeros((d, LANES - n_heads2), BF16)], axis=1)
            lane_pad = lambda v: jnp.pad(_group_dir_head(v.reshape(-1)),
                                         (0, LANES - n_heads2)).reshape(1, LANES)
            dt_bias, a_log = lane_pad(ssd_dt_bias[j]), lane_pad(ssd_a_log[j])
            expand = (jnp.arange(LANES)[:, None] == jnp.arange(n_heads2 * hd)[None, :] // hd
                      ).astype(BF16)
            conv_b = ssd_conv_b[j].reshape(1, -1)
            d_lanes = jnp.repeat(ssd_d[j], hd).reshape(1, g * h * hd)
            w_out = ssd_w_out[j].astype(BF16)

            def mixer(hin, shift, scale, s0, name):
                z, xconv, bc, dt_raw = _ssd_in_proj(hin, norm1_w[i], shift, scale, w_in,
                                                    ssd_conv_w[j], conv_b, name=name)
                dtf = _dt_factors(dt_raw, dt_bias, a_log, expand)
                return _ssd_scan(z, xconv, bc, dtf, d_lanes, s0)

            zeros = jnp.zeros((bsz, g, 2, n, h * hd), F32)
            yc, s_ctx = mixer(hc, csh1, csc1, zeros, "ssd_in_ctx")
            yl, _ = mixer(hl, sh1, sc1, s_ctx, "ssd_in")
            hl = _norm_matmul(yl, ssd_norm_w[j], w_out, post="resid", res=hl, gate=g1,
                              name="ssd_out")
            if not last:
                hc = _norm_matmul(yc, ssd_norm_w[j], w_out, post="resid", res=hc, gate=cg1,
                                  name="ssd_out_ctx")
        else:
            w1 = conf_w_pw1[j].astype(BF16)
            glu = _norm_matmul(hl, norm1_w[i], w1, shift=sh1, scale=sc1, bias=conf_b_pw1[j],
                               post="glu", name="conf_pw1")
            hl_new = _conformer_tail(glu, conf_w_dw[j], conf_b_dw[j], conf_ln_w[j], conf_ln_b[j],
                                     conf_w_pw2[j].astype(BF16), conf_b_pw2[j], hl, g1)
            if not last:
                gluc = _norm_matmul(hc, norm1_w[i], w1, shift=csh1, scale=csc1,
                                    bias=conf_b_pw1[j], post="glu", name="conf_pw1_ctx")
                hc = _conformer_tail(gluc, conf_w_dw[j], conf_b_dw[j], conf_ln_w[j], conf_ln_b[j],
                                     conf_w_pw2[j].astype(BF16), conf_b_pw2[j], hc, cg1)
            hl = hl_new
        w_up = ffn_w_up[i].astype(BF16)
        w_down = ffn_w_down[i].astype(BF16)
        hl = _conv_ffn(hl, norm2_w[i], sh2, sc2, w_up, ffn_conv_w[i], ffn_conv_b[i], w_down, g2,
                       on_grid=True, final_norm_w=final_norm_w if last else None)
        if not last:
            hc = _conv_ffn(hc, norm2_w[i], csh2, csc2, w_up, ffn_conv_w[i], ffn_conv_b[i], w_down,
                           cg2, on_grid=False, name="conv_ffn_ctx")
    return hl
```

```python
import functools

import jax
import jax.numpy as jnp
from jax import lax
from jax.experimental import pallas as pl
from jax.experimental.pallas import tpu as pltpu

F32 = jnp.float32
BF16 = jnp.bfloat16

EPS = 1e-6
LANES = 128
SUBLANES = 8
GRID_W = 64
SSD_GROUPS = 8
SSD_HPG = 4
SSD_HEAD_DIM = 64
SSD_STATE = 128
SSD_CHUNK = 128
SSD_CONV = 5
CONF_KERNEL = 31
FFN_CONV = 3

VMEM_LIMIT_BYTES = 56 * 1024 * 1024


def _params(*semantics):
    return pltpu.CompilerParams(dimension_semantics=semantics,
                                vmem_limit_bytes=VMEM_LIMIT_BYTES)


def _sigmoid(v):
    return 0.5 * jnp.tanh(0.5 * v) + 0.5


def _silu(v):
    return v * _sigmoid(v)


def _softplus(v):
    return jnp.maximum(v, 0.0) + jnp.log1p(jnp.exp(-jnp.abs(v)))


def _split3(v):
    v1 = v.astype(BF16)
    r = v - v1.astype(F32)
    v2 = r.astype(BF16)
    r = r - v2.astype(F32)
    return v1, v2, r.astype(BF16)


def _rows_from(ext, start, n):
    total = ext.shape[0]
    if start % SUBLANES == 0:
        return ext[start:start + n]
    lo = (start // SUBLANES) * SUBLANES
    rolled = pltpu.roll(ext, (lo - start) % total, 0)
    return rolled[lo:lo + n]


def _resident(shape):
    nd = len(shape)
    return pl.BlockSpec(shape, lambda *_: (0,) * nd, pipeline_mode=pl.Buffered(1))


def _ada_kernel(cond_ref, w_ref, b_ref, o_ref):
    a = _silu(cond_ref[...]).astype(BF16)
    o_ref[0] = jnp.dot(a, w_ref[0].astype(BF16), preferred_element_type=F32) + b_ref[0]


def _ada_params(cond, mod_w, mod_b, tn=1536):
    depth, d, n = mod_w.shape
    rows = cond.shape[0]
    return pl.pallas_call(
        _ada_kernel,
        grid=(depth, n // tn),
        in_specs=[pl.BlockSpec((rows, d), lambda i, j: (0, 0)),
                  pl.BlockSpec((1, d, tn), lambda i, j: (i, 0, j)),
                  pl.BlockSpec((1, 1, tn), lambda i, j: (i, 0, j))],
        out_specs=pl.BlockSpec((1, rows, tn), lambda i, j: (i, 0, j)),
        out_shape=jax.ShapeDtypeStruct((depth, rows, n), F32),
        compiler_params=_params("parallel", "parallel"),
        name="ada_params",
    )(cond, mod_w, mod_b.reshape(depth, 1, n))


def _mm_kernel(*refs, modulate, post, has_bias, n_out, n_chunk):
    it = iter(refs)
    x_ref = next(it)
    g_ref = next(it)
    sh_ref = sc_ref = b_ref = res_ref = gate_ref = None
    if modulate:
        sh_ref, sc_ref = next(it), next(it)
    w_ref = next(it)
    if has_bias:
        b_ref = next(it)
    if post == "resid":
        res_ref, gate_ref = next(it), next(it)
    o_ref = next(it)

    x = x_ref[0].astype(F32)
    ms = jnp.mean(x * x, axis=-1, keepdims=True)
    y = (x * lax.rsqrt(ms + EPS)) * g_ref[...]
    if modulate:
        y = y * (1.0 + sc_ref[0]) + sh_ref[0]
    yb = y.astype(BF16)

    def proj(c0):
        acc = jnp.dot(yb, w_ref[:, c0:c0 + n_chunk], preferred_element_type=F32)
        if has_bias:
            acc = acc + b_ref[:, c0:c0 + n_chunk]
        return acc

    for n0 in range(0, n_out, n_chunk):
        if post == "glu":
            out = proj(n0) * _sigmoid(proj(n_out + n0))
        elif post == "resid":
            out = res_ref[0, :, n0:n0 + n_chunk] + gate_ref[0, :, n0:n0 + n_chunk] * proj(n0)
        else:
            out = proj(n0)
        o_ref[0, :, n0:n0 + n_chunk] = out.astype(o_ref.dtype)


def _norm_matmul(x, g, w, *, shift=None, scale=None, bias=None, post="store",
                 res=None, gate=None, out_dtype=F32, tm=512, n_chunk=1024, name=None):
    bsz, t, k = x.shape
    nw = w.shape[1]
    n_out = nw // 2 if post == "glu" else nw
    tm = min(tm, t)
    n_chunk = max(c for c in range(LANES, n_chunk + 1, LANES) if n_out % c == 0)
    assert t % tm == 0
    modulate = shift is not None
    row = lambda b, i: (b, i, 0)
    per_batch = lambda b, i: (b, 0, 0)
    args = [x, g.reshape(1, k)]
    specs = [pl.BlockSpec((1, tm, k), row), _resident((1, k))]
    if modulate:
        args += [shift, scale]
        specs += [pl.BlockSpec((1, 1, k), per_batch)] * 2
    args.append(w)
    specs.append(_resident((k, nw)))
    if bias is not None:
        args.append(bias.reshape(1, nw))
        specs.append(_resident((1, nw)))
    if post == "resid":
        args += [res, gate]
        specs += [pl.BlockSpec((1, tm, n_out), row), pl.BlockSpec((1, 1, n_out), per_batch)]
    kern = functools.partial(_mm_kernel, modulate=modulate, post=post,
                             has_bias=bias is not None, n_out=n_out, n_chunk=n_chunk)
    return pl.pallas_call(
        kern,
        grid=(bsz, t // tm),
        in_specs=specs,
        out_specs=pl.BlockSpec((1, tm, n_out), row),
        out_shape=jax.ShapeDtypeStruct((bsz, t, n_out), out_dtype),
        compiler_params=_params("parallel", "parallel"),
        name=name,
    )(*args)


def _dt_kernel(raw_ref, bias_ref, alog_ref, e_ref,
               cumc_ref, wc_ref, eac_ref, dtr_ref, cumr_ref, dec_ref, tot_ref, *, seq):
    q = SSD_CHUNK
    nc = seq // q
    ii = lax.broadcasted_iota(jnp.int32, (q, q), 0)
    jj = lax.broadcasted_iota(jnp.int32, (q, q), 1)
    lower_b = jnp.where(jj <= ii, 1.0, 0.0).astype(BF16)
    upper_b = jnp.where(jj >= ii, 1.0, 0.0).astype(BF16)
    lane = lax.broadcasted_iota(jnp.int32, (1, LANES), 1)
    backward = (lane // SSD_HPG) % 2 == 1
    bias = bias_ref[...]
    a_neg = -jnp.exp(alog_ref[...])

    def body(ci, carry):
        rows = pl.ds(pl.multiple_of(ci * q, q), q)
        dt = _softplus(raw_ref[0, rows, :].astype(F32) + bias)
        parts = _split3(dt * a_neg)
        cum_f = sum(jnp.dot(lower_b, p, preferred_element_type=F32) for p in parts)
        cum_b = sum(jnp.dot(upper_b, p, preferred_element_type=F32) for p in parts)
        cum = jnp.where(backward, cum_b, cum_f)
        tot = cum_f[q - 1:q, :]
        cumc_ref[0, rows, :] = cum
        wc_ref[0, rows, :] = (dt * jnp.exp(tot - cum)).astype(BF16)
        eac_ref[0, rows, :] = jnp.exp(cum).astype(BF16)
        dtr_ref[0, :, rows] = dt.T
        cumr_ref[0, :, rows] = cum.T
        tot_ref[pl.ds(ci, 1), :] = tot
        return carry

    lax.fori_loop(0, nc, body, 0)
    dec = jnp.exp(tot_ref[...])
    dec_ref[0] = sum(jnp.dot(p, e_ref[...], preferred_element_type=F32) for p in _split3(dec))


def _dt_factors(proj, bias, alog, expand):
    bsz, seq, width = proj.shape
    nc = seq // SSD_CHUNK
    blk = width // LANES - 1
    n_exp = expand.shape[1]
    col = pl.BlockSpec((1, seq, LANES), lambda b: (b, 0, 0))
    rowb = pl.BlockSpec((1, LANES, seq), lambda b: (b, 0, 0))
    return pl.pallas_call(
        functools.partial(_dt_kernel, seq=seq),
        grid=(bsz,),
        in_specs=[pl.BlockSpec((1, seq, LANES), lambda b: (b, 0, blk)),
                  _resident((1, LANES)), _resident((1, LANES)), _resident((LANES, n_exp))],
        out_specs=[col, col, col, rowb, rowb,
                   pl.BlockSpec((1, nc, n_exp), lambda b: (b, 0, 0))],
        out_shape=[jax.ShapeDtypeStruct((bsz, seq, LANES), F32),
                   jax.ShapeDtypeStruct((bsz, seq, LANES), BF16),
                   jax.ShapeDtypeStruct((bsz, seq, LANES), BF16),
                   jax.ShapeDtypeStruct((bsz, LANES, seq), F32),
                   jax.ShapeDtypeStruct((bsz, LANES, seq), F32),
                   jax.ShapeDtypeStruct((bsz, nc, n_exp), F32)],
        scratch_shapes=[pltpu.VMEM((nc, LANES), F32)],
        compiler_params=_params("parallel"),
        name="ssd_dt",
    )(proj, bias, alog, expand)


def _ssd_kernel(x_ref, b_ref, c_ref, z_ref, cumc_ref, wc_ref, eac_ref, dtr_ref, cumr_ref,
                dec_ref, d_ref, s0_ref,
                y_ref, sn_ref,
                sc_ref, sent_ref, ccol_ref, st_ref, e_ref, *, seq):
    q = SSD_CHUNK
    nc = seq // q
    nh, hd = SSD_HPG, SSD_HEAD_DIM
    xw = nh * hd
    g = pl.program_id(1)

    kk = lax.broadcasted_iota(jnp.int32, (LANES, xw), 0)
    head_of_lane = lax.broadcasted_iota(jnp.int32, (LANES, xw), 1) // hd
    for d in range(2):
        e_ref[d] = jnp.where(kk == g * 2 * nh + d * nh + head_of_lane, 1.0, 0.0).astype(BF16)

    for gg in range(SSD_GROUPS):
        @pl.when(g == gg)
        def _():
            def cp_body(ci, carry):
                rows = pl.ds(pl.multiple_of(ci * q, q), q)
                ccol_ref[rows, :] = cumc_ref[0, rows, gg * 2 * nh:(gg + 1) * 2 * nh]
                return carry
            lax.fori_loop(0, nc, cp_body, 0, unroll=True)

    ii = lax.broadcasted_iota(jnp.int32, (q, q), 0)
    jj = lax.broadcasted_iota(jnp.int32, (q, q), 1)
    lane_head = lax.broadcasted_iota(jnp.int32, (q, xw), 1) // hd

    def intra(ci, carry):
        rows = pl.ds(pl.multiple_of(ci * q, q), q)
        xc = x_ref[0, rows, :]
        x_b = xc.astype(BF16)
        btb = b_ref[0, rows, :].astype(F32).T.astype(BF16)
        cbf = c_ref[0, rows, :]
        cb = jnp.dot(cbf, btb, preferred_element_type=F32)
        cum8 = ccol_ref[rows, :]
        lm = []
        for h in range(nh):
            acf, acb = cum8[:, h:h + 1], cum8[:, nh + h:nh + h + 1]
            arf = cumr_ref[0, h:h + 1, rows]
            arb = cumr_ref[0, nh + h:nh + h + 1, rows]
            dtf = dtr_ref[0, h:h + 1, rows]
            dtb = dtr_ref[0, nh + h:nh + h + 1, rows]
            arg = jnp.where(jj <= ii, acf - arf, acb - arb)
            dts = jnp.where(jj < ii, dtf, jnp.where(jj > ii, dtb, dtf + dtb))
            lm.append((jnp.exp(arg) * (cb * dts)).astype(BF16))
        masked = [jnp.where(lane_head == h, x_b, jnp.zeros_like(x_b)) for h in range(nh)]
        yd = (jnp.dot(jnp.concatenate(lm[0:2], axis=1), jnp.concatenate(masked[0:2], axis=0),
                      preferred_element_type=F32)
              + jnp.dot(jnp.concatenate(lm[2:4], axis=1), jnp.concatenate(masked[2:4], axis=0),
                        preferred_element_type=F32))
        y_ref[0, rows, :] = d_ref[...] * xc + yd
        for d in range(2):
            wexp = jnp.dot(wc_ref[0, rows, :], e_ref[d], preferred_element_type=F32)
            sc_ref[ci, d] = jnp.dot(btb, (xc * wexp).astype(BF16), preferred_element_type=F32)
        return carry

    lax.fori_loop(0, nc, intra, 0, unroll=min(4, nc))

    st_ref[...] = s0_ref[0, 0]

    def carry_states(c, carry):
        for d in range(2):
            ci = c if d == 0 else nc - 1 - c
            s_in = st_ref[d]
            sent_ref[ci, d] = s_in.astype(BF16)
            dec = dec_ref[0, pl.ds(ci, 1), d * xw:(d + 1) * xw]
            st_ref[d] = dec * s_in + sc_ref[ci, d]
        return carry

    lax.fori_loop(0, nc, carry_states, 0)
    sn_ref[0, 0] = st_ref[...]

    def inter(ci, carry):
        rows = pl.ds(pl.multiple_of(ci * q, q), q)
        cbf = c_ref[0, rows, :]
        y = y_ref[0, rows, :]
        for d in range(2):
            ea = jnp.dot(eac_ref[0, rows, :], e_ref[d], preferred_element_type=F32)
            y = y + ea * jnp.dot(cbf, sent_ref[ci, d], preferred_element_type=F32)
        y_ref[0, rows, :] = (y * _silu(z_ref[0, rows, :].astype(F32))).astype(y_ref.dtype)
        return carry

    lax.fori_loop(0, nc, inter, 0, unroll=min(4, nc))


def _ssd_scan(z, xconv, bc, dtf, d_lanes, s0):
    cumc, wc, eac, dtr, cumr, dec = dtf
    bsz, seq, di = z.shape
    g, h, hd, n = SSD_GROUPS, SSD_HPG, SSD_HEAD_DIM, SSD_STATE
    q = SSD_CHUNK
    nc = seq // q
    xw = h * hd
    state_spec = pl.BlockSpec((1, 1, 2, n, xw), lambda b, i: (b, i, 0, 0, 0))
    chan = pl.BlockSpec((1, seq, xw), lambda b, i: (b, 0, i))
    col = pl.BlockSpec((1, seq, LANES), lambda b, i: (b, 0, 0))
    rowb = pl.BlockSpec((1, 2 * h, seq), lambda b, i: (b, i, 0))
    kern = functools.partial(_ssd_kernel, seq=seq)
    return pl.pallas_call(
        kern,
        grid=(bsz, g),
        in_specs=[
            chan,
            pl.BlockSpec((1, seq, n), lambda b, i: (b, 0, i)),
            pl.BlockSpec((1, seq, n), lambda b, i: (b, 0, g + i)),
            chan,
            col, col, col, rowb, rowb,
            pl.BlockSpec((1, nc, 2 * xw), lambda b, i: (b, 0, i)),
            pl.BlockSpec((1, xw), lambda b, i: (0, i)),
            state_spec,
        ],
        out_specs=[chan, state_spec],
        out_shape=[jax.ShapeDtypeStruct((bsz, seq, di), F32),
                   jax.ShapeDtypeStruct((bsz, g, 2, n, xw), F32)],
        scratch_shapes=[pltpu.VMEM((nc, 2, n, xw), F32),
                        pltpu.VMEM((nc, 2, n, xw), BF16), pltpu.VMEM((seq, 2 * h), F32),
                        pltpu.VMEM((2, n, xw), F32), pltpu.VMEM((2, LANES, xw), BF16)],
        compiler_params=_params("parallel", "parallel"),
        name="ssd_scan",
    )(xconv, bc, bc, z, cumc, wc, eac, dtr, cumr, dec, d_lanes, s0)


def _ssd_in_kernel(top_ref, mid_ref, bot_ref, g_ref, sh_ref, sc_ref, w_ref, cw_ref, cb_ref,
                   z_ref, x_ref, bc_ref, dt_ref, yb_ref, win_ref, *, tm, rb, cw, nz):
    halo = SUBLANES
    t = pl.program_id(1)
    nt = pl.num_programs(1)
    di = z_ref.shape[2]

    def modulated(v):
        v = v.astype(F32)
        y = (v * lax.rsqrt(jnp.mean(v * v, axis=-1, keepdims=True) + EPS)) * g_ref[...]
        return (y * (1.0 + sc_ref[0]) + sh_ref[0]).astype(BF16)

    yb_ref[0:halo, :] = modulated(top_ref[0])
    yb_ref[halo:halo + tm, :] = modulated(mid_ref[0])
    yb_ref[halo + tm:2 * halo + tm, :] = modulated(bot_ref[0])
    top_ok = t > 0
    bot_ok = t < nt - 1

    dt0 = w_ref.shape[1] - LANES
    dt_ref[0] = jnp.dot(yb_ref[halo:halo + tm, :], w_ref[:, dt0:], preferred_element_type=F32)

    n_conv = 2 * di
    z_every = (n_conv // cw) // (di // nz)
    for ci, c0 in enumerate(range(0, n_conv, cw)):
        buf = ci % 2
        if ci % z_every == 0:
            n0 = (ci // z_every) * nz
            z_ref[0, :, n0:n0 + nz] = jnp.dot(yb_ref[halo:halo + tm, :], w_ref[:, n0:n0 + nz],
                                              preferred_element_type=F32).astype(z_ref.dtype)
        p = jnp.dot(yb_ref[...], w_ref[:, di + c0:di + c0 + cw], preferred_element_type=F32)
        win_ref[buf, 0:halo, :] = jnp.where(top_ok, p[0:halo], 0.0)
        win_ref[buf, halo:halo + tm, :] = p[halo:halo + tm]
        win_ref[buf, halo + tm:2 * halo + tm, :] = jnp.where(bot_ok, p[halo + tm:], 0.0)
        w = cw_ref[:, c0:c0 + cw]
        bias = cb_ref[:, c0:c0 + cw]
        for r0 in range(0, tm, rb):
            ext = win_ref[buf, r0:r0 + rb + 2 * halo, :]
            acc = bias + w[SSD_CONV // 2:SSD_CONV // 2 + 1, :] * ext[halo:halo + rb]
            for k in range(SSD_CONV):
                if k != SSD_CONV // 2:
                    acc = acc + w[k:k + 1, :] * _rows_from(ext, halo - SSD_CONV // 2 + k, rb)
            out = _silu(acc)
            if c0 < di:
                x_ref[0, r0:r0 + rb, c0:c0 + cw] = out
            else:
                bc_ref[0, r0:r0 + rb, c0 - di:c0 - di + cw] = out.astype(bc_ref.dtype)


def _ssd_in_proj(hin, g, shift, scale, w_in, conv_w, conv_b, *, tm=512, rb=128, cw=256, nz=512,
                 name="ssd_in"):
    bsz, t, d = hin.shape
    nw = w_in.shape[1]
    di = (nw - LANES) // 3
    tm = min(tm, t)
    halo = SUBLANES
    nh = tm // halo
    last_h = t // halo - 1
    row = lambda b, i: (b, i, 0)
    per_batch = lambda b, i: (b, 0, 0)
    kern = functools.partial(_ssd_in_kernel, tm=tm, rb=rb, cw=cw, nz=nz)
    return pl.pallas_call(
        kern,
        grid=(bsz, t // tm),
        in_specs=[
            pl.BlockSpec((1, halo, d), lambda b, i: (b, jnp.maximum(i * nh - 1, 0), 0)),
            pl.BlockSpec((1, tm, d), row),
            pl.BlockSpec((1, halo, d), lambda b, i: (b, jnp.minimum((i + 1) * nh, last_h), 0)),
            _resident((1, d)),
            pl.BlockSpec((1, 1, d), per_batch), pl.BlockSpec((1, 1, d), per_batch),
            _resident((d, nw)), _resident((SSD_CONV, 2 * di)), _resident((1, 2 * di)),
        ],
        out_specs=[pl.BlockSpec((1, tm, di), row), pl.BlockSpec((1, tm, di), row),
                   pl.BlockSpec((1, tm, di), row), pl.BlockSpec((1, tm, LANES), row)],
        out_shape=[jax.ShapeDtypeStruct((bsz, t, di), F32),
                   jax.ShapeDtypeStruct((bsz, t, di), F32),
                   jax.ShapeDtypeStruct((bsz, t, di), BF16),
                   jax.ShapeDtypeStruct((bsz, t, LANES), F32)],
        scratch_shapes=[pltpu.VMEM((tm + 2 * halo, d), BF16),
                        pltpu.VMEM((2, tm + 2 * halo, cw), F32)],
        compiler_params=_params("parallel", "parallel"),
        name=name,
    )(hin, hin, hin, g.reshape(1, d), shift, scale, w_in, conv_w, conv_b)


def _conf_kernel(top_ref, mid_ref, bot_ref, g_ref, sh_ref, sc_ref, w1_ref, b1_ref, wdw_ref,
                 bdw_ref, lnw_ref, lnb_ref, w2_ref, b2_ref, gate_ref, o_ref,
                 yb_ref, win_ref, cv_ref, *, tm, rb, halo, cw):
    t = pl.program_id(1)
    nt = pl.num_programs(1)
    ch = mid_ref.shape[2]
    pad = CONF_KERNEL // 2
    ext_rows = rb + SUBLANES

    def modulated(v):
        v = v.astype(F32)
        y = (v * lax.rsqrt(jnp.mean(v * v, axis=-1, keepdims=True) + EPS)) * g_ref[...]
        return (y * (1.0 + sc_ref[0]) + sh_ref[0]).astype(BF16)

    yb_ref[0:halo, :] = modulated(top_ref[0])
    yb_ref[halo:halo + tm, :] = modulated(mid_ref[0])
    yb_ref[halo + tm:2 * halo + tm, :] = modulated(bot_ref[0])
    top_ok = t > 0
    bot_ok = t < nt - 1

    for ci, c0 in enumerate(range(0, ch, cw)):
        buf = ci % 2
        a = jnp.dot(yb_ref[...], w1_ref[:, c0:c0 + cw], preferred_element_type=F32)
        gg = jnp.dot(yb_ref[...], w1_ref[:, ch + c0:ch + c0 + cw], preferred_element_type=F32)
        glu = (a + b1_ref[:, c0:c0 + cw]) * _sigmoid(gg + b1_ref[:, ch + c0:ch + c0 + cw])
        win_ref[buf, 0:halo, :] = jnp.where(top_ok, glu[0:halo], 0.0)
        win_ref[buf, halo:halo + tm, :] = glu[halo:halo + tm]
        win_ref[buf, halo + tm:2 * halo + tm, :] = jnp.where(bot_ok, glu[halo + tm:], 0.0)
        for l0 in range(0, cw, LANES):
            cs = slice(c0 + l0, c0 + l0 + LANES)
            for r0 in range(0, tm, rb):
                acc = jnp.zeros((rb, LANES), F32) + bdw_ref[:, cs]
                for r in range(SUBLANES):
                    part = None
                    for a8 in range(-2, 2):
                        k = SUBLANES * a8 + r + pad
                        if 0 <= k < CONF_KERNEL:
                            start = halo + r0 + SUBLANES * a8
                            term = (wdw_ref[k:k + 1, cs]
                                    * win_ref[buf, start:start + ext_rows, l0:l0 + LANES])
                            part = term if part is None else part + term
                    acc = acc + _rows_from(part, r, rb)
                cv_ref[r0:r0 + rb, cs] = acc

    hcv = cv_ref[...]
    mu = jnp.mean(hcv, axis=-1, keepdims=True)
    dv = hcv - mu
    yn = dv * lax.rsqrt(jnp.mean(dv * dv, axis=-1, keepdims=True) + EPS)
    yn = _silu(yn * lnw_ref[...] + lnb_ref[...])
    out = jnp.dot(yn.astype(BF16), w2_ref[...], preferred_element_type=F32) + b2_ref[...]
    o_ref[0] = mid_ref[0].astype(F32) + gate_ref[0] * out


def _conformer(hin, g, shift, scale, w1, b1, w_dw, b_dw, ln_w, ln_b, w2, b2, gate, *,
               tm=256, rb=128, halo=16, cw=256, name="conformer"):
    bsz, t, ch = hin.shape
    nh = tm // halo
    last_h = t // halo - 1
    row = lambda b, i: (b, i, 0)
    per_batch = lambda b, i: (b, 0, 0)
    kern = functools.partial(_conf_kernel, tm=tm, rb=rb, halo=halo, cw=cw)
    return pl.pallas_call(
        kern,
        grid=(bsz, t // tm),
        in_specs=[
            pl.BlockSpec((1, halo, ch), lambda b, i: (b, jnp.maximum(i * nh - 1, 0), 0)),
            pl.BlockSpec((1, tm, ch), row),
            pl.BlockSpec((1, halo, ch), lambda b, i: (b, jnp.minimum((i + 1) * nh, last_h), 0)),
            _resident((1, ch)),
            pl.BlockSpec((1, 1, ch), per_batch), pl.BlockSpec((1, 1, ch), per_batch),
            _resident((ch, 2 * ch)), _resident((1, 2 * ch)),
            _resident((CONF_KERNEL, ch)), _resident((1, ch)), _resident((1, ch)),
            _resident((1, ch)), _resident((ch, ch)), _resident((1, ch)),
            pl.BlockSpec((1, 1, ch), per_batch),
        ],
        out_specs=pl.BlockSpec((1, tm, ch), row),
        out_shape=jax.ShapeDtypeStruct((bsz, t, ch), F32),
        scratch_shapes=[pltpu.VMEM((tm + 2 * halo, ch), BF16),
                        pltpu.VMEM((2, tm + 2 * halo, cw), F32), pltpu.VMEM((tm, ch), F32)],
        compiler_params=_params("parallel", "parallel"),
        name=name,
    )(hin, hin, hin, g.reshape(1, ch), shift, scale, w1, b1.reshape(1, 2 * ch), w_dw,
      b_dw.reshape(1, ch), ln_w.reshape(1, ch), ln_b.reshape(1, ch), w2, b2.reshape(1, ch), gate)


def _ffn_kernel(*refs, tm, rb, on_grid, final_norm, hc):
    if final_norm:
        (top_ref, mid_ref, bot_ref, g_ref, sh_ref, sc_ref, wup_ref, cw_ref, cb_ref, wd_ref,
         gate_ref, fn_ref, o_ref, yb_ref, win_ref, val_ref, act_ref) = refs
    else:
        (top_ref, mid_ref, bot_ref, g_ref, sh_ref, sc_ref, wup_ref, cw_ref, cb_ref, wd_ref,
         gate_ref, o_ref, yb_ref, win_ref, val_ref, act_ref) = refs
        fn_ref = None
    halo = GRID_W
    guard = SUBLANES
    t = pl.program_id(1)
    nt = pl.num_programs(1)
    hid = wd_ref.shape[0]

    def modulated(v):
        v = v.astype(F32)
        y = (v * lax.rsqrt(jnp.mean(v * v, axis=-1, keepdims=True) + EPS)) * g_ref[...]
        return (y * (1.0 + sc_ref[0]) + sh_ref[0]).astype(BF16)

    yb_ref[0:halo, :] = modulated(top_ref[0])
    yb_ref[halo:halo + tm, :] = modulated(mid_ref[0])
    yb_ref[halo + tm:2 * halo + tm, :] = modulated(bot_ref[0])
    top_ok = t > 0
    bot_ok = t < nt - 1

    base = guard + halo
    for buf in range(2):
        win_ref[buf, 0:guard, :] = jnp.zeros((guard, hc), F32)
        win_ref[buf, base + tm + halo:base + tm + halo + guard, :] = jnp.zeros((guard, hc), F32)
    col = lax.broadcasted_iota(jnp.int32, (rb, hc), 0) % GRID_W
    row_taps = (-1, 0, 1) if on_grid else (0,)
    ext_rows = rb + 2 * guard
    n_rb = tm // rb
    n_chunks = hid // hc

    def up_piece(j, r):
        buf = j % 2
        c0 = j * hc
        m0 = halo + r * rb
        val_ref[buf, r * rb:(r + 1) * rb, :] = jnp.dot(
            yb_ref[m0:m0 + rb, :], wup_ref[:, c0:c0 + hc], preferred_element_type=F32)
        lo = 0 if r == 0 else m0
        hi = 2 * halo + tm if r == n_rb - 1 else m0 + rb
        gx = jnp.dot(yb_ref[lo:hi, :], wup_ref[:, hid + c0:hid + c0 + hc],
                     preferred_element_type=F32)
        if r == 0:
            win_ref[buf, guard:base, :] = jnp.where(top_ok, gx[0:halo], 0.0)
            gx = gx[halo:]
        if r == n_rb - 1:
            win_ref[buf, base + tm:base + tm + halo, :] = jnp.where(bot_ok, gx[rb:], 0.0)
            gx = gx[:rb]
        win_ref[buf, guard + m0:guard + m0 + rb, :] = gx

    def conv_piece(j, r):
        buf = j % 2
        c0 = j * hc
        parts = []
        for dc in (-1, 0, 1):
            acc = None
            for dr in row_taps:
                start = base + r * rb + dr * GRID_W - guard
                w = cw_ref[(dr + 1) * FFN_CONV + (dc + 1), :, c0:c0 + hc]
                term = w * win_ref[buf, start:start + ext_rows, :]
                acc = term if acc is None else acc + term
            parts.append(_rows_from(acc, guard + dc, rb))
        if on_grid:
            conv = (parts[1] + jnp.where(col >= 1, parts[0], 0.0)
                    + jnp.where(col <= GRID_W - 2, parts[2], 0.0))
        else:
            conv = parts[0] + parts[1] + parts[2]
        val = val_ref[buf, r * rb:(r + 1) * rb, :]
        act_ref[r * rb:(r + 1) * rb, c0:c0 + hc] = (
            _silu(conv + cb_ref[:, c0:c0 + hc]) * val).astype(BF16)

    for r in range(n_rb):
        up_piece(0, r)
    split = (n_chunks // 2) * hc
    down = None
    for j in range(n_chunks):
        for r in range(n_rb):
            conv_piece(j, r)
            if j + 1 < n_chunks:
                up_piece(j + 1, r)
        if (j + 1) * hc == split:
            down = jnp.dot(act_ref[:, 0:split], wd_ref[0:split, :], preferred_element_type=F32)
    down = down + jnp.dot(act_ref[:, split:], wd_ref[split:, :], preferred_element_type=F32)
    hn = mid_ref[0].astype(F32) + gate_ref[0] * down
    if final_norm:
        hn = (hn * lax.rsqrt(jnp.mean(hn * hn, axis=-1, keepdims=True) + EPS)) * fn_ref[...]
    o_ref[0] = hn


def _conv_ffn(hin, g, shift, scale, w_up, conv_w, conv_b, w_down, gate, *, on_grid,
              final_norm_w=None, tm=512, rb=128, hc=256, name="conv_ffn"):
    bsz, t, d = hin.shape
    hid = w_down.shape[0]
    tm = min(tm, t)
    halo = GRID_W
    nh = tm // halo
    last_h = t // halo - 1
    row = lambda b, i: (b, i, 0)
    per_batch = lambda b, i: (b, 0, 0)
    args = [hin, hin, hin, g.reshape(1, d), shift, scale, w_up,
            conv_w.reshape(FFN_CONV * FFN_CONV, 1, hid), conv_b.reshape(1, hid), w_down, gate]
    specs = [
        pl.BlockSpec((1, halo, d), lambda b, i: (b, jnp.maximum(i * nh - 1, 0), 0)),
        pl.BlockSpec((1, tm, d), row),
        pl.BlockSpec((1, halo, d), lambda b, i: (b, jnp.minimum((i + 1) * nh, last_h), 0)),
        _resident((1, d)),
        pl.BlockSpec((1, 1, d), per_batch), pl.BlockSpec((1, 1, d), per_batch),
        _resident((d, 2 * hid)),
        _resident((FFN_CONV * FFN_CONV, 1, hid)), _resident((1, hid)), _resident((hid, d)),
        pl.BlockSpec((1, 1, d), per_batch),
    ]
    if final_norm_w is not None:
        args.append(final_norm_w.reshape(1, d))
        specs.append(_resident((1, d)))
    kern = functools.partial(_ffn_kernel, tm=tm, on_grid=on_grid,
                             final_norm=final_norm_w is not None, hc=hc, rb=rb)
    return pl.pallas_call(
        kern,
        grid=(bsz, t // tm),
        in_specs=specs,
        out_specs=pl.BlockSpec((1, tm, d), row),
        out_shape=jax.ShapeDtypeStruct((bsz, t, d), F32),
        scratch_shapes=[pltpu.VMEM((tm + 2 * halo, d), BF16),
                        pltpu.VMEM((2, tm + 2 * halo + 2 * SUBLANES, hc), F32),
                        pltpu.VMEM((2, tm, hc), F32), pltpu.VMEM((tm, hid), BF16)],
        compiler_params=_params("parallel", "parallel"),
        name=name,
    )(*args)


def _group_dir_head(v):
    lead = v.shape[:-1]
    v = v.reshape(lead + (2, SSD_GROUPS, SSD_HPG))
    return jnp.swapaxes(v, -3, -2).reshape(lead + (2 * SSD_GROUPS * SSD_HPG,))


def kernel(x, c, ctx, c_ctx, mod_w, mod_b, norm1_w, norm2_w, ssd_w_in, ssd_conv_w, ssd_conv_b, ssd_dt_bias, ssd_a_log, ssd_d, ssd_norm_w, ssd_w_out, conf_w_pw1, conf_b_pw1, conf_w_dw, conf_b_dw, conf_ln_w, conf_ln_b, conf_w_pw2, conf_b_pw2, ffn_w_up, ffn_conv_w, ffn_conv_b, ffn_w_down, final_norm_w):
    bsz, seq, d = x.shape
    depth = mod_w.shape[0]
    g, h, hd, n = SSD_GROUPS, SSD_HPG, SSD_HEAD_DIM, SSD_STATE
    n_heads2 = 2 * g * h

    cond = jnp.concatenate([c, c_ctx[None, :], jnp.zeros((16 - bsz - 1, d), F32)], axis=0)
    mods = _ada_params(cond, mod_w, mod_b).reshape(depth, 16, 6, d)

    def latent_mod(i, k):
        return mods[i, :bsz, k][:, None, :]

    def context_mod(i, k):
        return jnp.broadcast_to(mods[i, bsz, k][None, None, :], (bsz, 1, d))

    hl, hc = x, ctx
    for i in range(depth):
        kind = i % 2
        j = i // 2
        last = i == depth - 1
        need_ctx = (not last) or kind == 0
        sh1, sc1, g1, sh2, sc2, g2 = (latent_mod(i, k) for k in range(6))
        if need_ctx:
            csh1, csc1, cg1, csh2, csc2, cg2 = (context_mod(i, k) for k in range(6))
        if kind == 0:
            w_in = ssd_w_in[j]
            dt0 = w_in.shape[1] - n_heads2
            w_in = jnp.concatenate(
                [w_in[:, :dt0].astype(BF16), _group_dir_head(w_in[:, dt0:]).astype(BF16),
                 jnp.zeros((d, LANES - n_heads2), BF16)], axis=1)
            lane_pad = lambda v: jnp.pad(_group_dir_head(v.reshape(-1)),
                                         (0, LANES - n_heads2)).reshape(1, LANES)
            dt_bias, a_log = lane_pad(ssd_dt_bias[j]), lane_pad(ssd_a_log[j])
            expand = (jnp.arange(LANES)[:, None] == jnp.arange(n_heads2 * hd)[None, :] // hd
                      ).astype(BF16)
            conv_b = ssd_conv_b[j].reshape(1, -1)
            d_lanes = jnp.repeat(ssd_d[j], hd).reshape(1, g * h * hd)
            w_out = ssd_w_out[j].astype(BF16)

            def mixer(hin, shift, scale, s0, name):
                z, xconv, bc, dt_raw = _ssd_in_proj(hin, norm1_w[i], shift, scale, w_in,
                                                    ssd_conv_w[j], conv_b, name=name)
                dtf = _dt_factors(dt_raw, dt_bias, a_log, expand)
                return _ssd_scan(z, xconv, bc, dtf, d_lanes, s0)

            zeros = jnp.zeros((bsz, g, 2, n, h * hd), F32)
            yc, s_ctx = mixer(hc, csh1, csc1, zeros, "ssd_in_ctx")
            yl, _ = mixer(hl, sh1, sc1, s_ctx, "ssd_in")
            hl = _norm_matmul(yl, ssd_norm_w[j], w_out, post="resid", res=hl, gate=g1,
                              name="ssd_out")
            if not last:
                hc = _norm_matmul(yc, ssd_norm_w[j], w_out, post="resid", res=hc, gate=cg1,
                                  name="ssd_out_ctx")
        else:
            conf = (conf_w_pw1[j].astype(BF16), conf_b_pw1[j], conf_w_dw[j], conf_b_dw[j],
                    conf_ln_w[j], conf_ln_b[j], conf_w_pw2[j].astype(BF16), conf_b_pw2[j])
            hl = _conformer(hl, norm1_w[i], sh1, sc1, *conf, g1)
            if not last:
                hc = _conformer(hc, norm1_w[i], csh1, csc1, *conf, cg1, name="conformer_ctx")
        w_up = ffn_w_up[i].astype(BF16)
        w_down = ffn_w_down[i].astype(BF16)
        hl = _conv_ffn(hl, norm2_w[i], sh2, sc2, w_up, ffn_conv_w[i], ffn_conv_b[i], w_down, g2,
                       on_grid=True, final_norm_w=final_norm_w if last else None)
        if not last:
            hc = _conv_ffn(hc, norm2_w[i], csh2, csc2, w_up, ffn_conv_w[i], ffn_conv_b[i], w_down,
                           cg2, on_grid=False, name="conv_ffn_ctx")
    return hl
```

```python
import functools

import jax
import jax.numpy as jnp
from jax import lax
from jax.experimental import pallas as pl
from jax.experimental.pallas import tpu as pltpu

F32 = jnp.float32
BF16 = jnp.bfloat16

EPS = 1e-6
LANES = 128
SUBLANES = 8
GRID_W = 64
SSD_GROUPS = 8
SSD_HPG = 4
SSD_HEAD_DIM = 64
SSD_STATE = 128
SSD_CHUNK = 128
SSD_CONV = 5
CONF_KERNEL = 31
FFN_CONV = 3

VMEM_LIMIT_BYTES = 56 * 1024 * 1024


def _params(*semantics):
    return pltpu.CompilerParams(dimension_semantics=semantics,
                                vmem_limit_bytes=VMEM_LIMIT_BYTES)


def _sigmoid(v):
    return 0.5 * jnp.tanh(0.5 * v) + 0.5


def _silu(v):
    return v * _sigmoid(v)


def _softplus(v):
    return jnp.maximum(v, 0.0) + jnp.log1p(jnp.exp(-jnp.abs(v)))


def _split3(v):
    v1 = v.astype(BF16)
    r = v - v1.astype(F32)
    v2 = r.astype(BF16)
    r = r - v2.astype(F32)
    return v1, v2, r.astype(BF16)


def _rows_from(ext, start, n):
    total = ext.shape[0]
    if start % SUBLANES == 0:
        return ext[start:start + n]
    lo = (start // SUBLANES) * SUBLANES
    rolled = pltpu.roll(ext, (lo - start) % total, 0)
    return rolled[lo:lo + n]


def _resident(shape):
    nd = len(shape)
    return pl.BlockSpec(shape, lambda *_: (0,) * nd, pipeline_mode=pl.Buffered(1))


def _ada_kernel(cond_ref, w_ref, b_ref, o_ref):
    a = _silu(cond_ref[...]).astype(BF16)
    o_ref[0] = jnp.dot(a, w_ref[0].astype(BF16), preferred_element_type=F32) + b_ref[0]


def _ada_params(cond, mod_w, mod_b, tn=1536):
    depth, d, n = mod_w.shape
    rows = cond.shape[0]
    return pl.pallas_call(
        _ada_kernel,
        grid=(depth, n // tn),
        in_specs=[pl.BlockSpec((rows, d), lambda i, j: (0, 0)),
                  pl.BlockSpec((1, d, tn), lambda i, j: (i, 0, j)),
                  pl.BlockSpec((1, 1, tn), lambda i, j: (i, 0, j))],
        out_specs=pl.BlockSpec((1, rows, tn), lambda i, j: (i, 0, j)),
        out_shape=jax.ShapeDtypeStruct((depth, rows, n), F32),
        compiler_params=_params("parallel", "parallel"),
        name="ada_params",
    )(cond, mod_w, mod_b.reshape(depth, 1, n))


def _mm_kernel(*refs, modulate, post, has_bias, n_out, n_chunk):
    it = iter(refs)
    x_ref = next(it)
    g_ref = next(it)
    sh_ref = sc_ref = b_ref = res_ref = gate_ref = None
    if modulate:
        sh_ref, sc_ref = next(it), next(it)
    w_ref = next(it)
    if has_bias:
        b_ref = next(it)
    if post == "resid":
        res_ref, gate_ref = next(it), next(it)
    o_ref = next(it)

    x = x_ref[0].astype(F32)
    ms = jnp.mean(x * x, axis=-1, keepdims=True)
    y = (x * lax.rsqrt(ms + EPS)) * g_ref[...]
    if modulate:
        y = y * (1.0 + sc_ref[0]) + sh_ref[0]
    yb = y.astype(BF16)

    def proj(c0):
        acc = jnp.dot(yb, w_ref[:, c0:c0 + n_chunk], preferred_element_type=F32)
        if has_bias:
            acc = acc + b_ref[:, c0:c0 + n_chunk]
        return acc

    for n0 in range(0, n_out, n_chunk):
        if post == "glu":
            out = proj(n0) * _sigmoid(proj(n_out + n0))
        elif post == "resid":
            out = res_ref[0, :, n0:n0 + n_chunk] + gate_ref[0, :, n0:n0 + n_chunk] * proj(n0)
        else:
            out = proj(n0)
        o_ref[0, :, n0:n0 + n_chunk] = out.astype(o_ref.dtype)


def _norm_matmul(x, g, w, *, shift=None, scale=None, bias=None, post="store",
                 res=None, gate=None, out_dtype=F32, tm=512, n_chunk=1024, name=None):
    bsz, t, k = x.shape
    nw = w.shape[1]
    n_out = nw // 2 if post == "glu" else nw
    tm = min(tm, t)
    n_chunk = max(c for c in range(LANES, n_chunk + 1, LANES) if n_out % c == 0)
    assert t % tm == 0
    modulate = shift is not None
    row = lambda b, i: (b, i, 0)
    per_batch = lambda b, i: (b, 0, 0)
    args = [x, g.reshape(1, k)]
    specs = [pl.BlockSpec((1, tm, k), row), _resident((1, k))]
    if modulate:
        args += [shift, scale]
        specs += [pl.BlockSpec((1, 1, k), per_batch)] * 2
    args.append(w)
    specs.append(_resident((k, nw)))
    if bias is not None:
        args.append(bias.reshape(1, nw))
        specs.append(_resident((1, nw)))
    if post == "resid":
        args += [res, gate]
        specs += [pl.BlockSpec((1, tm, n_out), row), pl.BlockSpec((1, 1, n_out), per_batch)]
    kern = functools.partial(_mm_kernel, modulate=modulate, post=post,
                             has_bias=bias is not None, n_out=n_out, n_chunk=n_chunk)
    return pl.pallas_call(
        kern,
        grid=(bsz, t // tm),
        in_specs=specs,
        out_specs=pl.BlockSpec((1, tm, n_out), row),
        out_shape=jax.ShapeDtypeStruct((bsz, t, n_out), out_dtype),
        compiler_params=_params("parallel", "parallel"),
        name=name,
    )(*args)


def _dt_kernel(raw_ref, bias_ref, alog_ref, e_ref,
               cumc_ref, wc_ref, eac_ref, dtr_ref, cumr_ref, dec_ref, tot_ref, *, seq):
    q = SSD_CHUNK
    nc = seq // q
    ii = lax.broadcasted_iota(jnp.int32, (q, q), 0)
    jj = lax.broadcasted_iota(jnp.int32, (q, q), 1)
    lower_b = jnp.where(jj <= ii, 1.0, 0.0).astype(BF16)
    upper_b = jnp.where(jj >= ii, 1.0, 0.0).astype(BF16)
    lane = lax.broadcasted_iota(jnp.int32, (1, LANES), 1)
    backward = (lane // SSD_HPG) % 2 == 1
    bias = bias_ref[...]
    a_neg = -jnp.exp(alog_ref[...])

    def body(ci, carry):
        rows = pl.ds(pl.multiple_of(ci * q, q), q)
        dt = _softplus(raw_ref[0, rows, :].astype(F32) + bias)
        parts = _split3(dt * a_neg)
        cum_f = sum(jnp.dot(lower_b, p, preferred_element_type=F32) for p in parts)
        cum_b = sum(jnp.dot(upper_b, p, preferred_element_type=F32) for p in parts)
        cum = jnp.where(backward, cum_b, cum_f)
        tot = cum_f[q - 1:q, :]
        cumc_ref[0, rows, :] = cum
        wc_ref[0, rows, :] = (dt * jnp.exp(tot - cum)).astype(BF16)
        eac_ref[0, rows, :] = jnp.exp(cum).astype(BF16)
        dtr_ref[0, :, rows] = dt.T
        cumr_ref[0, :, rows] = cum.T
        tot_ref[pl.ds(ci, 1), :] = tot
        return carry

    lax.fori_loop(0, nc, body, 0)
    dec = jnp.exp(tot_ref[...])
    dec_ref[0] = sum(jnp.dot(p, e_ref[...], preferred_element_type=F32) for p in _split3(dec))


def _dt_factors(proj, bias, alog, expand):
    bsz, seq, width = proj.shape
    nc = seq // SSD_CHUNK
    blk = width // LANES - 1
    n_exp = expand.shape[1]
    col = pl.BlockSpec((1, seq, LANES), lambda b: (b, 0, 0))
    rowb = pl.BlockSpec((1, LANES, seq), lambda b: (b, 0, 0))
    return pl.pallas_call(
        functools.partial(_dt_kernel, seq=seq),
        grid=(bsz,),
        in_specs=[pl.BlockSpec((1, seq, LANES), lambda b: (b, 0, blk)),
                  _resident((1, LANES)), _resident((1, LANES)), _resident((LANES, n_exp))],
        out_specs=[col, col, col, rowb, rowb,
                   pl.BlockSpec((1, nc, n_exp), lambda b: (b, 0, 0))],
        out_shape=[jax.ShapeDtypeStruct((bsz, seq, LANES), F32),
                   jax.ShapeDtypeStruct((bsz, seq, LANES), BF16),
                   jax.ShapeDtypeStruct((bsz, seq, LANES), BF16),
                   jax.ShapeDtypeStruct((bsz, LANES, seq), F32),
                   jax.ShapeDtypeStruct((bsz, LANES, seq), F32),
                   jax.ShapeDtypeStruct((bsz, nc, n_exp), F32)],
        scratch_shapes=[pltpu.VMEM((nc, LANES), F32)],
        compiler_params=_params("parallel"),
        name="ssd_dt",
    )(proj, bias, alog, expand)


def _ssd_kernel(*refs, seq, zero_init):
    if zero_init:
        (x_ref, bt_ref, c_ref, z_ref, cumc_ref, wc_ref, eac_ref, dtr_ref, cumr_ref, dec_ref,
         d_ref, y_ref, sn_ref, sc_ref, sent_ref, ccol_ref, st_ref, e_ref) = refs
        s0_ref = None
    else:
        (x_ref, bt_ref, c_ref, z_ref, cumc_ref, wc_ref, eac_ref, dtr_ref, cumr_ref, dec_ref,
         d_ref, s0_ref, y_ref, sn_ref, sc_ref, sent_ref, ccol_ref, st_ref, e_ref) = refs
    q = SSD_CHUNK
    nc = seq // q
    nh, hd = SSD_HPG, SSD_HEAD_DIM
    xw = nh * hd
    g = pl.program_id(1)

    kk = lax.broadcasted_iota(jnp.int32, (LANES, xw), 0)
    head_of_lane = lax.broadcasted_iota(jnp.int32, (LANES, xw), 1) // hd
    for d in range(2):
        e_ref[d] = jnp.where(kk == g * 2 * nh + d * nh + head_of_lane, 1.0, 0.0).astype(BF16)

    for gg in range(SSD_GROUPS):
        @pl.when(g == gg)
        def _():
            def cp_body(ci, carry):
                rows = pl.ds(pl.multiple_of(ci * q, q), q)
                ccol_ref[rows, :] = cumc_ref[0, rows, gg * 2 * nh:(gg + 1) * 2 * nh]
                return carry
            lax.fori_loop(0, nc, cp_body, 0, unroll=True)

    ii = lax.broadcasted_iota(jnp.int32, (q, q), 0)
    jj = lax.broadcasted_iota(jnp.int32, (q, q), 1)
    lane_head = lax.broadcasted_iota(jnp.int32, (q, xw), 1) // hd

    def intra(ci, carry):
        rows = pl.ds(pl.multiple_of(ci * q, q), q)
        xc = x_ref[0, rows, :]
        x_b = xc.astype(BF16)
        btb = bt_ref[0, 0, :, rows]
        cbf = c_ref[0, rows, :]
        cb = jnp.dot(cbf, btb, preferred_element_type=F32)
        cum8 = ccol_ref[rows, :]
        lm = []
        for h in range(nh):
            acf, acb = cum8[:, h:h + 1], cum8[:, nh + h:nh + h + 1]
            arf = cumr_ref[0, h:h + 1, rows]
            arb = cumr_ref[0, nh + h:nh + h + 1, rows]
            dtf = dtr_ref[0, h:h + 1, rows]
            dtb = dtr_ref[0, nh + h:nh + h + 1, rows]
            arg = jnp.where(jj <= ii, acf - arf, acb - arb)
            dts = jnp.where(jj < ii, dtf, jnp.where(jj > ii, dtb, dtf + dtb))
            lm.append((jnp.exp(arg) * (cb * dts)).astype(BF16))
        masked = [jnp.where(lane_head == h, x_b, jnp.zeros_like(x_b)) for h in range(nh)]
        yd = (jnp.dot(jnp.concatenate(lm[0:2], axis=1), jnp.concatenate(masked[0:2], axis=0),
                      preferred_element_type=F32)
              + jnp.dot(jnp.concatenate(lm[2:4], axis=1), jnp.concatenate(masked[2:4], axis=0),
                        preferred_element_type=F32))
        y_ref[0, rows, :] = d_ref[...] * xc + yd
        for d in range(2):
            wexp = jnp.dot(wc_ref[0, rows, :], e_ref[d], preferred_element_type=F32)
            sc_ref[ci, d] = jnp.dot(btb, (xc * wexp).astype(BF16), preferred_element_type=F32)
        return carry

    lax.fori_loop(0, nc, intra, 0, unroll=min(4, nc))

    st_ref[...] = jnp.zeros(st_ref.shape, F32) if zero_init else s0_ref[0, 0]

    def carry_states(c, carry):
        for d in range(2):
            ci = c if d == 0 else nc - 1 - c
            s_in = st_ref[d]
            sent_ref[ci, d] = s_in.astype(BF16)
            dec = dec_ref[0, pl.ds(ci, 1), d * xw:(d + 1) * xw]
            st_ref[d] = dec * s_in + sc_ref[ci, d]
        return carry

    lax.fori_loop(0, nc, carry_states, 0)
    sn_ref[0, 0] = st_ref[...]

    def inter(ci, carry):
        rows = pl.ds(pl.multiple_of(ci * q, q), q)
        cbf = c_ref[0, rows, :]
        y = y_ref[0, rows, :]
        for d in range(2):
            ea = jnp.dot(eac_ref[0, rows, :], e_ref[d], preferred_element_type=F32)
            y = y + ea * jnp.dot(cbf, sent_ref[ci, d], preferred_element_type=F32)
        y_ref[0, rows, :] = (y * _silu(z_ref[0, rows, :].astype(F32))).astype(y_ref.dtype)
        return carry

    lax.fori_loop(0, nc, inter, 0, unroll=min(4, nc))


def _ssd_scan(z, xconv, bt, cmat, dtf, d_lanes, s0):
    cumc, wc, eac, dtr, cumr, dec = dtf
    bsz, seq, di = z.shape
    g, h, hd, n = SSD_GROUPS, SSD_HPG, SSD_HEAD_DIM, SSD_STATE
    q = SSD_CHUNK
    nc = seq // q
    xw = h * hd
    state_spec = pl.BlockSpec((1, 1, 2, n, xw), lambda b, i: (b, i, 0, 0, 0))
    chan = pl.BlockSpec((1, seq, xw), lambda b, i: (b, 0, i))
    col = pl.BlockSpec((1, seq, LANES), lambda b, i: (b, 0, 0))
    rowb = pl.BlockSpec((1, 2 * h, seq), lambda b, i: (b, i, 0))
    kern = functools.partial(_ssd_kernel, seq=seq, zero_init=s0 is None)
    args = [xconv, bt, cmat, z, cumc, wc, eac, dtr, cumr, dec, d_lanes]
    specs = [
        chan,
        pl.BlockSpec((1, 1, n, seq), lambda b, i: (b, i, 0, 0)),
        pl.BlockSpec((1, seq, n), lambda b, i: (b, 0, i)),
        chan,
        col, col, col, rowb, rowb,
        pl.BlockSpec((1, nc, 2 * xw), lambda b, i: (b, 0, i)),
        pl.BlockSpec((1, xw), lambda b, i: (0, i)),
    ]
    if s0 is not None:
        args.append(s0)
        specs.append(state_spec)
    return pl.pallas_call(
        kern,
        grid=(bsz, g),
        in_specs=specs,
        out_specs=[chan, state_spec],
        out_shape=[jax.ShapeDtypeStruct((bsz, seq, di), F32),
                   jax.ShapeDtypeStruct((bsz, g, 2, n, xw), F32)],
        scratch_shapes=[pltpu.VMEM((nc, 2, n, xw), F32),
                        pltpu.VMEM((nc, 2, n, xw), BF16), pltpu.VMEM((seq, 2 * h), F32),
                        pltpu.VMEM((2, n, xw), F32), pltpu.VMEM((2, LANES, xw), BF16)],
        compiler_params=_params("parallel", "parallel"),
        name="ssd_scan",
    )(*args)


def _ssd_in_kernel(top_ref, mid_ref, bot_ref, g_ref, sh_ref, sc_ref, w_ref, cw_ref, cb_ref,
                   z_ref, x_ref, bt_ref, c_ref, dt_ref, yb_ref, win_ref, *, tm, rb, cw, nz):
    halo = SUBLANES
    t = pl.program_id(1)
    nt = pl.num_programs(1)
    di = z_ref.shape[2]
    n_state = c_ref.shape[2]

    def modulated(v):
        v = v.astype(F32)
        y = (v * lax.rsqrt(jnp.mean(v * v, axis=-1, keepdims=True) + EPS)) * g_ref[...]
        return (y * (1.0 + sc_ref[0]) + sh_ref[0]).astype(BF16)

    yb_ref[0:halo, :] = modulated(top_ref[0])
    yb_ref[halo:halo + tm, :] = modulated(mid_ref[0])
    yb_ref[halo + tm:2 * halo + tm, :] = modulated(bot_ref[0])
    top_ok = t > 0
    bot_ok = t < nt - 1

    dt0 = w_ref.shape[1] - LANES
    dt_ref[0] = jnp.dot(yb_ref[halo:halo + tm, :], w_ref[:, dt0:], preferred_element_type=F32)

    n_conv = 2 * di
    z_every = (n_conv // cw) // (di // nz)
    for ci, c0 in enumerate(range(0, n_conv, cw)):
        buf = ci % 2
        if ci % z_every == 0:
            n0 = (ci // z_every) * nz
            z_ref[0, :, n0:n0 + nz] = jnp.dot(yb_ref[halo:halo + tm, :], w_ref[:, n0:n0 + nz],
                                              preferred_element_type=F32).astype(z_ref.dtype)
        p = jnp.dot(yb_ref[...], w_ref[:, di + c0:di + c0 + cw], preferred_element_type=F32)
        win_ref[buf, 0:halo, :] = jnp.where(top_ok, p[0:halo], 0.0)
        win_ref[buf, halo:halo + tm, :] = p[halo:halo + tm]
        win_ref[buf, halo + tm:2 * halo + tm, :] = jnp.where(bot_ok, p[halo + tm:], 0.0)
        w = cw_ref[:, c0:c0 + cw]
        bias = cb_ref[:, c0:c0 + cw]
        for r0 in range(0, tm, rb):
            ext = win_ref[buf, r0:r0 + rb + 2 * halo, :]
            acc = bias + w[SSD_CONV // 2:SSD_CONV // 2 + 1, :] * ext[halo:halo + rb]
            for k in range(SSD_CONV):
                if k != SSD_CONV // 2:
                    acc = acc + w[k:k + 1, :] * _rows_from(ext, halo - SSD_CONV // 2 + k, rb)
            out = _silu(acc)
            if c0 < di:
                x_ref[0, r0:r0 + rb, c0:c0 + cw] = out
            elif c0 < di + n_state:
                for l0 in range(0, cw, SSD_STATE):
                    grp = (c0 - di + l0) // SSD_STATE
                    bt_ref[0, grp, :, r0:r0 + rb] = out[:, l0:l0 + SSD_STATE].T.astype(bt_ref.dtype)
            else:
                c1 = c0 - di - n_state
                c_ref[0, r0:r0 + rb, c1:c1 + cw] = out.astype(c_ref.dtype)


def _ssd_in_proj(hin, g, shift, scale, w_in, conv_w, conv_b, *, tm=512, rb=128, cw=256, nz=512,
                 name="ssd_in"):
    bsz, t, d = hin.shape
    nw = w_in.shape[1]
    di = (nw - LANES) // 3
    tm = min(tm, t)
    halo = SUBLANES
    nh = tm // halo
    last_h = t // halo - 1
    row = lambda b, i: (b, i, 0)
    per_batch = lambda b, i: (b, 0, 0)
    kern = functools.partial(_ssd_in_kernel, tm=tm, rb=rb, cw=cw, nz=nz)
    return pl.pallas_call(
        kern,
        grid=(bsz, t // tm),
        in_specs=[
            pl.BlockSpec((1, halo, d), lambda b, i: (b, jnp.maximum(i * nh - 1, 0), 0)),
            pl.BlockSpec((1, tm, d), row),
            pl.BlockSpec((1, halo, d), lambda b, i: (b, jnp.minimum((i + 1) * nh, last_h), 0)),
            _resident((1, d)),
            pl.BlockSpec((1, 1, d), per_batch), pl.BlockSpec((1, 1, d), per_batch),
            _resident((d, nw)), _resident((SSD_CONV, 2 * di)), _resident((1, 2 * di)),
        ],
        out_specs=[pl.BlockSpec((1, tm, di), row), pl.BlockSpec((1, tm, di), row),
                   pl.BlockSpec((1, SSD_GROUPS, SSD_STATE, tm), lambda b, i: (b, 0, 0, i)),
                   pl.BlockSpec((1, tm, di // 2), row), pl.BlockSpec((1, tm, LANES), row)],
        out_shape=[jax.ShapeDtypeStruct((bsz, t, di), F32),
                   jax.ShapeDtypeStruct((bsz, t, di), F32),
                   jax.ShapeDtypeStruct((bsz, SSD_GROUPS, SSD_STATE, t), BF16),
                   jax.ShapeDtypeStruct((bsz, t, di // 2), BF16),
                   jax.ShapeDtypeStruct((bsz, t, LANES), F32)],
        scratch_shapes=[pltpu.VMEM((tm + 2 * halo, d), BF16),
                        pltpu.VMEM((2, tm + 2 * halo, cw), F32)],
        compiler_params=_params("parallel", "parallel"),
        name=name,
    )(hin, hin, hin, g.reshape(1, d), shift, scale, w_in, conv_w, conv_b)


def _conf_kernel(top_ref, mid_ref, bot_ref, g_ref, sh_ref, sc_ref, w1_ref, b1_ref, wdw_ref,
                 bdw_ref, lnw_ref, lnb_ref, w2_ref, b2_ref, gate_ref, o_ref,
                 yb_ref, win_ref, cv_ref, *, tm, rb, halo, cw):
    t = pl.program_id(1)
    nt = pl.num_programs(1)
    ch = mid_ref.shape[2]
    pad = CONF_KERNEL // 2
    ext_rows = rb + SUBLANES

    def modulated(v):
        v = v.astype(F32)
        y = (v * lax.rsqrt(jnp.mean(v * v, axis=-1, keepdims=True) + EPS)) * g_ref[...]
        return (y * (1.0 + sc_ref[0]) + sh_ref[0]).astype(BF16)

    yb_ref[0:halo, :] = modulated(top_ref[0])
    yb_ref[halo:halo + tm, :] = modulated(mid_ref[0])
    yb_ref[halo + tm:2 * halo + tm, :] = modulated(bot_ref[0])
    top_ok = t > 0
    bot_ok = t < nt - 1

    for ci, c0 in enumerate(range(0, ch, cw)):
        buf = ci % 2
        a = jnp.dot(yb_ref[...], w1_ref[:, c0:c0 + cw], preferred_element_type=F32)
        gg = jnp.dot(yb_ref[...], w1_ref[:, ch + c0:ch + c0 + cw], preferred_element_type=F32)
        glu = (a + b1_ref[:, c0:c0 + cw]) * _sigmoid(gg + b1_ref[:, ch + c0:ch + c0 + cw])
        win_ref[buf, 0:halo, :] = jnp.where(top_ok, glu[0:halo], 0.0)
        win_ref[buf, halo:halo + tm, :] = glu[halo:halo + tm]
        win_ref[buf, halo + tm:2 * halo + tm, :] = jnp.where(bot_ok, glu[halo + tm:], 0.0)
        for l0 in range(0, cw, LANES):
            cs = slice(c0 + l0, c0 + l0 + LANES)
            for r0 in range(0, tm, rb):
                acc = jnp.zeros((rb, LANES), F32) + bdw_ref[:, cs]
                for r in range(SUBLANES):
                    part = None
                    for a8 in range(-2, 2):
                        k = SUBLANES * a8 + r + pad
                        if 0 <= k < CONF_KERNEL:
                            start = halo + r0 + SUBLANES * a8
                            term = (wdw_ref[k:k + 1, cs]
                                    * win_ref[buf, start:start + ext_rows, l0:l0 + LANES])
                            part = term if part is None else part + term
                    acc = acc + _rows_from(part, r, rb)
                cv_ref[r0:r0 + rb, cs] = acc

    hcv = cv_ref[...]
    mu = jnp.mean(hcv, axis=-1, keepdims=True)
    dv = hcv - mu
    yn = dv * lax.rsqrt(jnp.mean(dv * dv, axis=-1, keepdims=True) + EPS)
    yn = _silu(yn * lnw_ref[...] + lnb_ref[...])
    out = jnp.dot(yn.astype(BF16), w2_ref[...], preferred_element_type=F32) + b2_ref[...]
    o_ref[0] = mid_ref[0].astype(F32) + gate_ref[0] * out


def _conformer(hin, g, shift, scale, w1, b1, w_dw, b_dw, ln_w, ln_b, w2, b2, gate, *,
               tm=256, rb=128, halo=16, cw=256, name="conformer"):
    bsz, t, ch = hin.shape
    nh = tm // halo
    last_h = t // halo - 1
    row = lambda b, i: (b, i, 0)
    per_batch = lambda b, i: (b, 0, 0)
    kern = functools.partial(_conf_kernel, tm=tm, rb=rb, halo=halo, cw=cw)
    return pl.pallas_call(
        kern,
        grid=(bsz, t // tm),
        in_specs=[
            pl.BlockSpec((1, halo, ch), lambda b, i: (b, jnp.maximum(i * nh - 1, 0), 0)),
            pl.BlockSpec((1, tm, ch), row),
            pl.BlockSpec((1, halo, ch), lambda b, i: (b, jnp.minimum((i + 1) * nh, last_h), 0)),
            _resident((1, ch)),
            pl.BlockSpec((1, 1, ch), per_batch), pl.BlockSpec((1, 1, ch), per_batch),
            _resident((ch, 2 * ch)), _resident((1, 2 * ch)),
            _resident((CONF_KERNEL, ch)), _resident((1, ch)), _resident((1, ch)),
            _resident((1, ch)), _resident((ch, ch)), _resident((1, ch)),
            pl.BlockSpec((1, 1, ch), per_batch),
        ],
        out_specs=pl.BlockSpec((1, tm, ch), row),
        out_shape=jax.ShapeDtypeStruct((bsz, t, ch), F32),
        scratch_shapes=[pltpu.VMEM((tm + 2 * halo, ch), BF16),
                        pltpu.VMEM((2, tm + 2 * halo, cw), F32), pltpu.VMEM((tm, ch), F32)],
        compiler_params=_params("parallel", "parallel"),
        name=name,
    )(hin, hin, hin, g.reshape(1, ch), shift, scale, w1, b1.reshape(1, 2 * ch), w_dw,
      b_dw.reshape(1, ch), ln_w.reshape(1, ch), ln_b.reshape(1, ch), w2, b2.reshape(1, ch), gate)


def _ffn_kernel(*refs, tm, rb, on_grid, final_norm, hc):
    if final_norm:
        (top_ref, mid_ref, bot_ref, g_ref, sh_ref, sc_ref, wup_ref, cw_ref, cb_ref, wd_ref,
         gate_ref, fn_ref, o_ref, yb_ref, win_ref, val_ref, act_ref) = refs
    else:
        (top_ref, mid_ref, bot_ref, g_ref, sh_ref, sc_ref, wup_ref, cw_ref, cb_ref, wd_ref,
         gate_ref, o_ref, yb_ref, win_ref, val_ref, act_ref) = refs
        fn_ref = None
    halo = GRID_W
    guard = SUBLANES
    t = pl.program_id(1)
    nt = pl.num_programs(1)
    hid = wd_ref.shape[1]
    wup_ref = wup_ref.at[0]
    wd_ref = wd_ref.at[0]

    def modulated(v):
        v = v.astype(F32)
        y = (v * lax.rsqrt(jnp.mean(v * v, axis=-1, keepdims=True) + EPS)) * g_ref[...]
        return (y * (1.0 + sc_ref[0]) + sh_ref[0]).astype(BF16)

    yb_ref[0:halo, :] = modulated(top_ref[0])
    yb_ref[halo:halo + tm, :] = modulated(mid_ref[0])
    yb_ref[halo + tm:2 * halo + tm, :] = modulated(bot_ref[0])
    top_ok = t > 0
    bot_ok = t < nt - 1

    base = guard + halo
    for buf in range(2):
        win_ref[buf, 0:guard, :] = jnp.zeros((guard, hc), F32)
        win_ref[buf, base + tm + halo:base + tm + halo + guard, :] = jnp.zeros((guard, hc), F32)
    col = lax.broadcasted_iota(jnp.int32, (rb, hc), 0) % GRID_W
    row_taps = (-1, 0, 1) if on_grid else (0,)
    ext_rows = rb + 2 * guard
    n_rb = tm // rb
    n_chunks = hid // hc

    def up_piece(j, r):
        buf = j % 2
        c0 = j * hc
        m0 = halo + r * rb
        val_ref[buf, r * rb:(r + 1) * rb, :] = jnp.dot(
            yb_ref[m0:m0 + rb, :], wup_ref[:, c0:c0 + hc], preferred_element_type=F32)
        lo = 0 if r == 0 else m0
        hi = 2 * halo + tm if r == n_rb - 1 else m0 + rb
        gx = jnp.dot(yb_ref[lo:hi, :], wup_ref[:, hid + c0:hid + c0 + hc],
                     preferred_element_type=F32)
        if r == 0:
            win_ref[buf, guard:base, :] = jnp.where(top_ok, gx[0:halo], 0.0)
            gx = gx[halo:]
        if r == n_rb - 1:
            win_ref[buf, base + tm:base + tm + halo, :] = jnp.where(bot_ok, gx[rb:], 0.0)
            gx = gx[:rb]
        win_ref[buf, guard + m0:guard + m0 + rb, :] = gx

    def conv_piece(j, r):
        buf = j % 2
        c0 = j * hc
        parts = []
        for dc in (-1, 0, 1):
            acc = None
            for dr in row_taps:
                start = base + r * rb + dr * GRID_W - guard
                w = cw_ref[(dr + 1) * FFN_CONV + (dc + 1), :, c0:c0 + hc]
                term = w * win_ref[buf, start:start + ext_rows, :]
                acc = term if acc is None else acc + term
            parts.append(_rows_from(acc, guard + dc, rb))
        if on_grid:
            conv = (parts[1] + jnp.where(col >= 1, parts[0], 0.0)
                    + jnp.where(col <= GRID_W - 2, parts[2], 0.0))
        else:
            conv = parts[0] + parts[1] + parts[2]
        val = val_ref[buf, r * rb:(r + 1) * rb, :]
        act_ref[r * rb:(r + 1) * rb, c0:c0 + hc] = (
            _silu(conv + cb_ref[:, c0:c0 + hc]) * val).astype(BF16)

    for r in range(n_rb):
        up_piece(0, r)
    split = (n_chunks // 2) * hc
    down = None
    for j in range(n_chunks):
        for r in range(n_rb):
            conv_piece(j, r)
            if j + 1 < n_chunks:
                up_piece(j + 1, r)
        if (j + 1) * hc == split:
            down = jnp.dot(act_ref[:, 0:split], wd_ref[0:split, :], preferred_element_type=F32)
    down = down + jnp.dot(act_ref[:, split:], wd_ref[split:, :], preferred_element_type=F32)
    hn = mid_ref[0].astype(F32) + gate_ref[0] * down
    if final_norm:
        hn = (hn * lax.rsqrt(jnp.mean(hn * hn, axis=-1, keepdims=True) + EPS)) * fn_ref[...]
    o_ref[0] = hn


def _conv_ffn(hin, g, shift, scale, layer, w_up, conv_w, conv_b, w_down, gate, *, on_grid,
              final_norm_w=None, tm=512, rb=128, hc=256, name="conv_ffn"):
    bsz, t, d = hin.shape
    hid = w_down.shape[1]
    stacked = lambda shape: pl.BlockSpec((1,) + shape, lambda *_: (layer, 0, 0),
                                         pipeline_mode=pl.Buffered(1))
    tm = min(tm, t)
    halo = GRID_W
    nh = tm // halo
    last_h = t // halo - 1
    row = lambda b, i: (b, i, 0)
    per_batch = lambda b, i: (b, 0, 0)
    args = [hin, hin, hin, g.reshape(1, d), shift, scale, w_up,
            conv_w.reshape(FFN_CONV * FFN_CONV, 1, hid), conv_b.reshape(1, hid), w_down, gate]
    specs = [
        pl.BlockSpec((1, halo, d), lambda b, i: (b, jnp.maximum(i * nh - 1, 0), 0)),
        pl.BlockSpec((1, tm, d), row),
        pl.BlockSpec((1, halo, d), lambda b, i: (b, jnp.minimum((i + 1) * nh, last_h), 0)),
        _resident((1, d)),
        pl.BlockSpec((1, 1, d), per_batch), pl.BlockSpec((1, 1, d), per_batch),
        stacked((d, 2 * hid)),
        _resident((FFN_CONV * FFN_CONV, 1, hid)), _resident((1, hid)), stacked((hid, d)),
        pl.BlockSpec((1, 1, d), per_batch),
    ]
    if final_norm_w is not None:
        args.append(final_norm_w.reshape(1, d))
        specs.append(_resident((1, d)))
    kern = functools.partial(_ffn_kernel, tm=tm, on_grid=on_grid,
                             final_norm=final_norm_w is not None, hc=hc, rb=rb)
    return pl.pallas_call(
        kern,
        grid=(bsz, t // tm),
        in_specs=specs,
        out_specs=pl.BlockSpec((1, tm, d), row),
        out_shape=jax.ShapeDtypeStruct((bsz, t, d), F32),
        scratch_shapes=[pltpu.VMEM((tm + 2 * halo, d), BF16),
                        pltpu.VMEM((2, tm + 2 * halo + 2 * SUBLANES, hc), F32),
                        pltpu.VMEM((2, tm, hc), F32), pltpu.VMEM((tm, hid), BF16)],
        compiler_params=_params("parallel", "parallel"),
        name=name,
    )(*args)


def _group_dir_head(v):
    lead = v.shape[:-1]
    v = v.reshape(lead + (2, SSD_GROUPS, SSD_HPG))
    return jnp.swapaxes(v, -3, -2).reshape(lead + (2 * SSD_GROUPS * SSD_HPG,))


def kernel(x, c, ctx, c_ctx, mod_w, mod_b, norm1_w, norm2_w, ssd_w_in, ssd_conv_w, ssd_conv_b, ssd_dt_bias, ssd_a_log, ssd_d, ssd_norm_w, ssd_w_out, conf_w_pw1, conf_b_pw1, conf_w_dw, conf_b_dw, conf_ln_w, conf_ln_b, conf_w_pw2, conf_b_pw2, ffn_w_up, ffn_conv_w, ffn_conv_b, ffn_w_down, final_norm_w):
    bsz, seq, d = x.shape
    depth = mod_w.shape[0]
    g, h, hd, n = SSD_GROUPS, SSD_HPG, SSD_HEAD_DIM, SSD_STATE
    n_heads2 = 2 * g * h

    cond = jnp.concatenate([c, c_ctx[None, :], jnp.zeros((16 - bsz - 1, d), F32)], axis=0)
    mods = _ada_params(cond, mod_w, mod_b).reshape(depth, 16, 6, d)

    def latent_mod(i, k):
        return mods[i, :bsz, k][:, None, :]

    def context_mod(i, k):
        return jnp.broadcast_to(mods[i, bsz, k][None, None, :], (bsz, 1, d))

    w_up, w_down = ffn_w_up.astype(BF16), ffn_w_down.astype(BF16)
    hl, hc = x, ctx
    for i in range(depth):
        kind = i % 2
        j = i // 2
        last = i == depth - 1
        need_ctx = (not last) or kind == 0
        sh1, sc1, g1, sh2, sc2, g2 = (latent_mod(i, k) for k in range(6))
        if need_ctx:
            csh1, csc1, cg1, csh2, csc2, cg2 = (context_mod(i, k) for k in range(6))
        if kind == 0:
            w_in = ssd_w_in[j]
            dt0 = w_in.shape[1] - n_heads2
            w_in = jnp.concatenate(
                [w_in[:, :dt0].astype(BF16), _group_dir_head(w_in[:, dt0:]).astype(BF16),
                 jnp.zeros((d, LANES - n_heads2), BF16)], axis=1)
            lane_pad = lambda v: jnp.pad(_group_dir_head(v.reshape(-1)),
                                         (0, LANES - n_heads2)).reshape(1, LANES)
            dt_bias, a_log = lane_pad(ssd_dt_bias[j]), lane_pad(ssd_a_log[j])
            expand = (jnp.arange(LANES)[:, None] == jnp.arange(n_heads2 * hd)[None, :] // hd
                      ).astype(BF16)
            conv_b = ssd_conv_b[j].reshape(1, -1)
            d_lanes = jnp.repeat(ssd_d[j], hd).reshape(1, g * h * hd)
            w_out = ssd_w_out[j].astype(BF16)

            def mixer(hin, shift, scale, s0, name):
                z, xconv, bt, cmat, dt_raw = _ssd_in_proj(hin, norm1_w[i], shift, scale, w_in,
                                                          ssd_conv_w[j], conv_b, name=name)
                dtf = _dt_factors(dt_raw, dt_bias, a_log, expand)
                return _ssd_scan(z, xconv, bt, cmat, dtf, d_lanes, s0)

            yc, s_ctx = mixer(hc, csh1, csc1, None, "ssd_in_ctx")
            yl, _ = mixer(hl, sh1, sc1, s_ctx, "ssd_in")
            hl = _norm_matmul(yl, ssd_norm_w[j], w_out, post="resid", res=hl, gate=g1,
                              name="ssd_out")
            if not last:
                hc = _norm_matmul(yc, ssd_norm_w[j], w_out, post="resid", res=hc, gate=cg1,
                                  name="ssd_out_ctx")
        else:
            conf = (conf_w_pw1[j].astype(BF16), conf_b_pw1[j], conf_w_dw[j], conf_b_dw[j],
                    conf_ln_w[j], conf_ln_b[j], conf_w_pw2[j].astype(BF16), conf_b_pw2[j])
            hl = _conformer(hl, norm1_w[i], sh1, sc1, *conf, g1)
            if not last:
                hc = _conformer(hc, norm1_w[i], csh1, csc1, *conf, cg1, name="conformer_ctx")
        hl = _conv_ffn(hl, norm2_w[i], sh2, sc2, i, w_up, ffn_conv_w[i], ffn_conv_b[i], w_down, g2,
                       on_grid=True, final_norm_w=final_norm_w if last else None)
        if not last:
            hc = _conv_ffn(hc, norm2_w[i], csh2, csc2, i, w_up, ffn_conv_w[i], ffn_conv_b[i],
                           w_down, cg2, on_grid=False, name="conv_ffn_ctx")
    return hl
```

```python
import functools

import jax
import jax.numpy as jnp
from jax import lax
from jax.experimental import pallas as pl
from jax.experimental.pallas import tpu as pltpu

F32 = jnp.float32
BF16 = jnp.bfloat16

EPS = 1e-6
LANES = 128
SUBLANES = 8
GRID_W = 64
SSD_GROUPS = 8
SSD_HPG = 4
SSD_HEAD_DIM = 64
SSD_STATE = 128
SSD_CHUNK = 128
SSD_CONV = 5
CONF_KERNEL = 31
FFN_CONV = 3

VMEM_LIMIT_BYTES = 56 * 1024 * 1024


def _params(*semantics):
    return pltpu.CompilerParams(dimension_semantics=semantics,
                                vmem_limit_bytes=VMEM_LIMIT_BYTES)


def _sigmoid(v):
    return 0.5 * jnp.tanh(0.5 * v) + 0.5


def _silu(v):
    return v * _sigmoid(v)


def _softplus(v):
    return jnp.maximum(v, 0.0) + jnp.log1p(jnp.exp(-jnp.abs(v)))


def _split3(v):
    v1 = v.astype(BF16)
    r = v - v1.astype(F32)
    v2 = r.astype(BF16)
    r = r - v2.astype(F32)
    return v1, v2, r.astype(BF16)


def _rows_from(ext, start, n):
    total = ext.shape[0]
    if start % SUBLANES == 0:
        return ext[start:start + n]
    lo = (start // SUBLANES) * SUBLANES
    rolled = pltpu.roll(ext, (lo - start) % total, 0)
    return rolled[lo:lo + n]


def _resident(shape):
    nd = len(shape)
    return pl.BlockSpec(shape, lambda *_: (0,) * nd, pipeline_mode=pl.Buffered(1))


def _ada_kernel(cond_ref, w_ref, b_ref, o_ref):
    a = _silu(cond_ref[...]).astype(BF16)
    o_ref[0] = jnp.dot(a, w_ref[0].astype(BF16), preferred_element_type=F32) + b_ref[0]


def _ada_params(cond, mod_w, mod_b, tn=1536):
    depth, d, n = mod_w.shape
    rows = cond.shape[0]
    return pl.pallas_call(
        _ada_kernel,
        grid=(depth, n // tn),
        in_specs=[pl.BlockSpec((rows, d), lambda i, j: (0, 0)),
                  pl.BlockSpec((1, d, tn), lambda i, j: (i, 0, j)),
                  pl.BlockSpec((1, 1, tn), lambda i, j: (i, 0, j))],
        out_specs=pl.BlockSpec((1, rows, tn), lambda i, j: (i, 0, j)),
        out_shape=jax.ShapeDtypeStruct((depth, rows, n), F32),
        compiler_params=_params("parallel", "parallel"),
        name="ada_params",
    )(cond, mod_w, mod_b.reshape(depth, 1, n))


def _mm_kernel(*refs, modulate, post, has_bias, n_out, n_chunk):
    it = iter(refs)
    x_ref = next(it)
    g_ref = next(it)
    sh_ref = sc_ref = b_ref = res_ref = gate_ref = None
    if modulate:
        sh_ref, sc_ref = next(it), next(it)
    w_ref = next(it)
    if has_bias:
        b_ref = next(it)
    if post == "resid":
        res_ref, gate_ref = next(it), next(it)
    o_ref = next(it)

    x = x_ref[0].astype(F32)
    ms = jnp.mean(x * x, axis=-1, keepdims=True)
    y = (x * lax.rsqrt(ms + EPS)) * g_ref[...]
    if modulate:
        y = y * (1.0 + sc_ref[0]) + sh_ref[0]
    yb = y.astype(BF16)

    def proj(c0):
        acc = jnp.dot(yb, w_ref[:, c0:c0 + n_chunk], preferred_element_type=F32)
        if has_bias:
            acc = acc + b_ref[:, c0:c0 + n_chunk]
        return acc

    for n0 in range(0, n_out, n_chunk):
        if post == "glu":
            out = proj(n0) * _sigmoid(proj(n_out + n0))
        elif post == "resid":
            out = res_ref[0, :, n0:n0 + n_chunk] + gate_ref[0, :, n0:n0 + n_chunk] * proj(n0)
        else:
            out = proj(n0)
        o_ref[0, :, n0:n0 + n_chunk] = out.astype(o_ref.dtype)


def _norm_matmul(x, g, w, *, shift=None, scale=None, bias=None, post="store",
                 res=None, gate=None, out_dtype=F32, tm=512, n_chunk=1024, name=None):
    bsz, t, k = x.shape
    nw = w.shape[1]
    n_out = nw // 2 if post == "glu" else nw
    tm = min(tm, t)
    n_chunk = max(c for c in range(LANES, n_chunk + 1, LANES) if n_out % c == 0)
    assert t % tm == 0
    modulate = shift is not None
    row = lambda b, i: (b, i, 0)
    per_batch = lambda b, i: (b, 0, 0)
    args = [x, g.reshape(1, k)]
    specs = [pl.BlockSpec((1, tm, k), row), _resident((1, k))]
    if modulate:
        args += [shift, scale]
        specs += [pl.BlockSpec((1, 1, k), per_batch)] * 2
    args.append(w)
    specs.append(_resident((k, nw)))
    if bias is not None:
        args.append(bias.reshape(1, nw))
        specs.append(_resident((1, nw)))
    if post == "resid":
        args += [res, gate]
        specs += [pl.BlockSpec((1, tm, n_out), row), pl.BlockSpec((1, 1, n_out), per_batch)]
    kern = functools.partial(_mm_kernel, modulate=modulate, post=post,
                             has_bias=bias is not None, n_out=n_out, n_chunk=n_chunk)
    return pl.pallas_call(
        kern,
        grid=(bsz, t // tm),
        in_specs=specs,
        out_specs=pl.BlockSpec((1, tm, n_out), row),
        out_shape=jax.ShapeDtypeStruct((bsz, t, n_out), out_dtype),
        compiler_params=_params("parallel", "parallel"),
        name=name,
    )(*args)


def _dt_kernel(raw_ref, bias_ref, alog_ref, e_ref,
               cumc_ref, wc_ref, eac_ref, dtr_ref, cumr_ref, dec_ref, tot_ref, *, seq):
    q = SSD_CHUNK
    nc = seq // q
    ii = lax.broadcasted_iota(jnp.int32, (q, q), 0)
    jj = lax.broadcasted_iota(jnp.int32, (q, q), 1)
    lower_b = jnp.where(jj <= ii, 1.0, 0.0).astype(BF16)
    upper_b = jnp.where(jj >= ii, 1.0, 0.0).astype(BF16)
    lane = lax.broadcasted_iota(jnp.int32, (1, LANES), 1)
    backward = (lane // SSD_HPG) % 2 == 1
    bias = bias_ref[...]
    a_neg = -jnp.exp(alog_ref[...])

    def body(ci, carry):
        rows = pl.ds(pl.multiple_of(ci * q, q), q)
        dt = _softplus(raw_ref[0, rows, :].astype(F32) + bias)
        parts = _split3(dt * a_neg)
        cum_f = sum(jnp.dot(lower_b, p, preferred_element_type=F32) for p in parts)
        cum_b = sum(jnp.dot(upper_b, p, preferred_element_type=F32) for p in parts)
        cum = jnp.where(backward, cum_b, cum_f)
        tot = cum_f[q - 1:q, :]
        cumc_ref[0, rows, :] = cum
        wc_ref[0, rows, :] = (dt * jnp.exp(tot - cum)).astype(BF16)
        eac_ref[0, rows, :] = jnp.exp(cum).astype(BF16)
        dtr_ref[0, :, rows] = dt.T
        cumr_ref[0, :, rows] = cum.T
        tot_ref[pl.ds(ci, 1), :] = tot
        return carry

    lax.fori_loop(0, nc, body, 0, unroll=min(4, nc))
    dec = jnp.exp(tot_ref[...])
    dec_ref[0] = sum(jnp.dot(p, e_ref[...], preferred_element_type=F32) for p in _split3(dec))


def _dt_factors(proj, bias, alog, expand):
    bsz, seq, width = proj.shape
    nc = seq // SSD_CHUNK
    blk = width // LANES - 1
    n_exp = expand.shape[1]
    col = pl.BlockSpec((1, seq, LANES), lambda b: (b, 0, 0))
    rowb = pl.BlockSpec((1, LANES, seq), lambda b: (b, 0, 0))
    return pl.pallas_call(
        functools.partial(_dt_kernel, seq=seq),
        grid=(bsz,),
        in_specs=[pl.BlockSpec((1, seq, LANES), lambda b: (b, 0, blk)),
                  _resident((1, LANES)), _resident((1, LANES)), _resident((LANES, n_exp))],
        out_specs=[col, col, col, rowb, rowb,
                   pl.BlockSpec((1, nc, n_exp), lambda b: (b, 0, 0))],
        out_shape=[jax.ShapeDtypeStruct((bsz, seq, LANES), F32),
                   jax.ShapeDtypeStruct((bsz, seq, LANES), BF16),
                   jax.ShapeDtypeStruct((bsz, seq, LANES), BF16),
                   jax.ShapeDtypeStruct((bsz, LANES, seq), F32),
                   jax.ShapeDtypeStruct((bsz, LANES, seq), F32),
                   jax.ShapeDtypeStruct((bsz, nc, n_exp), F32)],
        scratch_shapes=[pltpu.VMEM((nc, LANES), F32)],
        compiler_params=_params("parallel"),
        name="ssd_dt",
    )(proj, bias, alog, expand)


def _ssd_kernel(*refs, seq, zero_init):
    if zero_init:
        (x_ref, bt_ref, c_ref, z_ref, cumc_ref, wc_ref, eac_ref, dtr_ref, cumr_ref, dec_ref,
         d_ref, y_ref, sn_ref, sc_ref, sent_ref, ccol_ref, st_ref, e_ref) = refs
        s0_ref = None
    else:
        (x_ref, bt_ref, c_ref, z_ref, cumc_ref, wc_ref, eac_ref, dtr_ref, cumr_ref, dec_ref,
         d_ref, s0_ref, y_ref, sn_ref, sc_ref, sent_ref, ccol_ref, st_ref, e_ref) = refs
    q = SSD_CHUNK
    nc = seq // q
    nh, hd = SSD_HPG, SSD_HEAD_DIM
    xw = nh * hd
    g = pl.program_id(1)

    kk = lax.broadcasted_iota(jnp.int32, (LANES, xw), 0)
    head_of_lane = lax.broadcasted_iota(jnp.int32, (LANES, xw), 1) // hd
    for d in range(2):
        e_ref[d] = jnp.where(kk == g * 2 * nh + d * nh + head_of_lane, 1.0, 0.0).astype(BF16)

    for gg in range(SSD_GROUPS):
        @pl.when(g == gg)
        def _():
            def cp_body(ci, carry):
                rows = pl.ds(pl.multiple_of(ci * q, q), q)
                ccol_ref[rows, :] = cumc_ref[0, rows, gg * 2 * nh:(gg + 1) * 2 * nh]
                return carry
            lax.fori_loop(0, nc, cp_body, 0, unroll=True)

    ii = lax.broadcasted_iota(jnp.int32, (q, q), 0)
    jj = lax.broadcasted_iota(jnp.int32, (q, q), 1)
    lane_head = lax.broadcasted_iota(jnp.int32, (q, xw), 1) // hd

    def intra(ci, carry):
        rows = pl.ds(pl.multiple_of(ci * q, q), q)
        xc = x_ref[0, rows, :]
        x_b = xc.astype(BF16)
        btb = bt_ref[0, 0, :, rows]
        cbf = c_ref[0, rows, :]
        cb = jnp.dot(cbf, btb, preferred_element_type=F32)
        cum8 = ccol_ref[rows, :]
        lm = []
        for h in range(nh):
            acf, acb = cum8[:, h:h + 1], cum8[:, nh + h:nh + h + 1]
            arf = cumr_ref[0, h:h + 1, rows]
            arb = cumr_ref[0, nh + h:nh + h + 1, rows]
            dtf = dtr_ref[0, h:h + 1, rows]
            dtb = dtr_ref[0, nh + h:nh + h + 1, rows]
            arg = jnp.where(jj <= ii, acf - arf, acb - arb)
            dts = jnp.where(jj < ii, dtf, jnp.where(jj > ii, dtb, dtf + dtb))
            lm.append((jnp.exp(arg) * (cb * dts)).astype(BF16))
        masked = [jnp.where(lane_head == h, x_b, jnp.zeros_like(x_b)) for h in range(nh)]
        yd = (jnp.dot(jnp.concatenate(lm[0:2], axis=1), jnp.concatenate(masked[0:2], axis=0),
                      preferred_element_type=F32)
              + jnp.dot(jnp.concatenate(lm[2:4], axis=1), jnp.concatenate(masked[2:4], axis=0),
                        preferred_element_type=F32))
        y_ref[0, rows, :] = d_ref[...] * xc + yd
        for d in range(2):
            wexp = jnp.dot(wc_ref[0, rows, :], e_ref[d], preferred_element_type=F32)
            sc_ref[ci, d] = jnp.dot(btb, (xc * wexp).astype(BF16), preferred_element_type=F32)
        return carry

    lax.fori_loop(0, nc, intra, 0, unroll=min(8, nc))

    st_ref[...] = jnp.zeros(st_ref.shape, F32) if zero_init else s0_ref[0, 0]

    def carry_states(c, carry):
        for d in range(2):
            ci = c if d == 0 else nc - 1 - c
            s_in = st_ref[d]
            sent_ref[ci, d] = s_in.astype(BF16)
            dec = dec_ref[0, pl.ds(ci, 1), d * xw:(d + 1) * xw]
            st_ref[d] = dec * s_in + sc_ref[ci, d]
        return carry

    lax.fori_loop(0, nc, carry_states, 0, unroll=2)
    sn_ref[0, 0] = st_ref[...]

    def inter(ci, carry):
        rows = pl.ds(pl.multiple_of(ci * q, q), q)
        cbf = c_ref[0, rows, :]
        y = y_ref[0, rows, :]
        for d in range(2):
            ea = jnp.dot(eac_ref[0, rows, :], e_ref[d], preferred_element_type=F32)
            y = y + ea * jnp.dot(cbf, sent_ref[ci, d], preferred_element_type=F32)
        y_ref[0, rows, :] = (y * _silu(z_ref[0, rows, :].astype(F32))).astype(y_ref.dtype)
        return carry

    lax.fori_loop(0, nc, inter, 0, unroll=min(4, nc))


def _ssd_scan(z, xconv, bt, cmat, dtf, d_lanes, s0):
    cumc, wc, eac, dtr, cumr, dec = dtf
    bsz, seq, di = z.shape
    g, h, hd, n = SSD_GROUPS, SSD_HPG, SSD_HEAD_DIM, SSD_STATE
    q = SSD_CHUNK
    nc = seq // q
    xw = h * hd
    state_spec = pl.BlockSpec((1, 1, 2, n, xw), lambda b, i: (b, i, 0, 0, 0))
    chan = pl.BlockSpec((1, seq, xw), lambda b, i: (b, 0, i))
    col = pl.BlockSpec((1, seq, LANES), lambda b, i: (b, 0, 0))
    rowb = pl.BlockSpec((1, 2 * h, seq), lambda b, i: (b, i, 0))
    kern = functools.partial(_ssd_kernel, seq=seq, zero_init=s0 is None)
    args = [xconv, bt, cmat, z, cumc, wc, eac, dtr, cumr, dec, d_lanes]
    specs = [
        chan,
        pl.BlockSpec((1, 1, n, seq), lambda b, i: (b, i, 0, 0)),
        pl.BlockSpec((1, seq, n), lambda b, i: (b, 0, i)),
        chan,
        col, col, col, rowb, rowb,
        pl.BlockSpec((1, nc, 2 * xw), lambda b, i: (b, 0, i)),
        pl.BlockSpec((1, xw), lambda b, i: (0, i)),
    ]
    if s0 is not None:
        args.append(s0)
        specs.append(state_spec)
    return pl.pallas_call(
        kern,
        grid=(bsz, g),
        in_specs=specs,
        out_specs=[chan, state_spec],
        out_shape=[jax.ShapeDtypeStruct((bsz, seq, di), F32),
                   jax.ShapeDtypeStruct((bsz, g, 2, n, xw), F32)],
        scratch_shapes=[pltpu.VMEM((nc, 2, n, xw), F32),
                        pltpu.VMEM((nc, 2, n, xw), BF16), pltpu.VMEM((seq, 2 * h), F32),
                        pltpu.VMEM((2, n, xw), F32), pltpu.VMEM((2, LANES, xw), BF16)],
        compiler_params=_params("parallel", "parallel"),
        name="ssd_scan",
    )(*args)


def _ssd_in_kernel(top_ref, mid_ref, bot_ref, g_ref, sh_ref, sc_ref, w_ref, cw_ref, cb_ref,
                   z_ref, x_ref, bt_ref, c_ref, dt_ref, yb_ref, win_ref, *, tm, rb, cw, nz):
    halo = SUBLANES
    t = pl.program_id(1)
    nt = pl.num_programs(1)
    di = z_ref.shape[2]
    n_state = c_ref.shape[2]

    def modulated(v):
        v = v.astype(F32)
        y = (v * lax.rsqrt(jnp.mean(v * v, axis=-1, keepdims=True) + EPS)) * g_ref[...]
        return (y * (1.0 + sc_ref[0]) + sh_ref[0]).astype(BF16)

    yb_ref[0:halo, :] = modulated(top_ref[0])
    yb_ref[halo:halo + tm, :] = modulated(mid_ref[0])
    yb_ref[halo + tm:2 * halo + tm, :] = modulated(bot_ref[0])
    top_ok = t > 0
    bot_ok = t < nt - 1

    dt0 = w_ref.shape[1] - LANES
    dt_ref[0] = jnp.dot(yb_ref[halo:halo + tm, :], w_ref[:, dt0:], preferred_element_type=F32)

    n_conv = 2 * di
    z_every = (n_conv // cw) // (di // nz)
    for ci, c0 in enumerate(range(0, n_conv, cw)):
        buf = ci % 2
        if ci % z_every == 0:
            n0 = (ci // z_every) * nz
            z_ref[0, :, n0:n0 + nz] = jnp.dot(yb_ref[halo:halo + tm, :], w_ref[:, n0:n0 + nz],
                                              preferred_element_type=F32).astype(z_ref.dtype)
        p = jnp.dot(yb_ref[...], w_ref[:, di + c0:di + c0 + cw], preferred_element_type=F32)
        win_ref[buf, 0:halo, :] = jnp.where(top_ok, p[0:halo], 0.0)
        win_ref[buf, halo:halo + tm, :] = p[halo:halo + tm]
        win_ref[buf, halo + tm:2 * halo + tm, :] = jnp.where(bot_ok, p[halo + tm:], 0.0)
        w = cw_ref[:, c0:c0 + cw]
        bias = cb_ref[:, c0:c0 + cw]
        for r0 in range(0, tm, rb):
            ext = win_ref[buf, r0:r0 + rb + 2 * halo, :]
            acc = bias + w[SSD_CONV // 2:SSD_CONV // 2 + 1, :] * ext[halo:halo + rb]
            for k in range(SSD_CONV):
                if k != SSD_CONV // 2:
                    acc = acc + w[k:k + 1, :] * _rows_from(ext, halo - SSD_CONV // 2 + k, rb)
            out = _silu(acc)
            if c0 < di:
                x_ref[0, r0:r0 + rb, c0:c0 + cw] = out
            elif c0 < di + n_state:
                for l0 in range(0, cw, SSD_STATE):
                    grp = (c0 - di + l0) // SSD_STATE
                    bt_ref[0, grp, :, r0:r0 + rb] = out[:, l0:l0 + SSD_STATE].T.astype(bt_ref.dtype)
            else:
                c1 = c0 - di - n_state
                c_ref[0, r0:r0 + rb, c1:c1 + cw] = out.astype(c_ref.dtype)


def _ssd_in_proj(hin, g, shift, scale, w_in, conv_w, conv_b, *, tm=512, rb=128, cw=256, nz=512,
                 name="ssd_in"):
    bsz, t, d = hin.shape
    nw = w_in.shape[1]
    di = (nw - LANES) // 3
    tm = min(tm, t)
    halo = SUBLANES
    nh = tm // halo
    last_h = t // halo - 1
    row = lambda b, i: (b, i, 0)
    per_batch = lambda b, i: (b, 0, 0)
    kern = functools.partial(_ssd_in_kernel, tm=tm, rb=rb, cw=cw, nz=nz)
    return pl.pallas_call(
        kern,
        grid=(bsz, t // tm),
        in_specs=[
            pl.BlockSpec((1, halo, d), lambda b, i: (b, jnp.maximum(i * nh - 1, 0), 0)),
            pl.BlockSpec((1, tm, d), row),
            pl.BlockSpec((1, halo, d), lambda b, i: (b, jnp.minimum((i + 1) * nh, last_h), 0)),
            _resident((1, d)),
            pl.BlockSpec((1, 1, d), per_batch), pl.BlockSpec((1, 1, d), per_batch),
            _resident((d, nw)), _resident((SSD_CONV, 2 * di)), _resident((1, 2 * di)),
        ],
        out_specs=[pl.BlockSpec((1, tm, di), row), pl.BlockSpec((1, tm, di), row),
                   pl.BlockSpec((1, SSD_GROUPS, SSD_STATE, tm), lambda b, i: (b, 0, 0, i)),
                   pl.BlockSpec((1, tm, di // 2), row), pl.BlockSpec((1, tm, LANES), row)],
        out_shape=[jax.ShapeDtypeStruct((bsz, t, di), F32),
                   jax.ShapeDtypeStruct((bsz, t, di), F32),
                   jax.ShapeDtypeStruct((bsz, SSD_GROUPS, SSD_STATE, t), BF16),
                   jax.ShapeDtypeStruct((bsz, t, di // 2), BF16),
                   jax.ShapeDtypeStruct((bsz, t, LANES), F32)],
        scratch_shapes=[pltpu.VMEM((tm + 2 * halo, d), BF16),
                        pltpu.VMEM((2, tm + 2 * halo, cw), F32)],
        compiler_params=_params("parallel", "parallel"),
        name=name,
    )(hin, hin, hin, g.reshape(1, d), shift, scale, w_in, conv_w, conv_b)


def _conf_kernel(top_ref, mid_ref, bot_ref, g_ref, sh_ref, sc_ref, w1_ref, b1_ref, wdw_ref,
                 bdw_ref, lnw_ref, lnb_ref, w2_ref, b2_ref, gate_ref, o_ref,
                 yb_ref, win_ref, cv_ref, *, tm, rb, halo, cw):
    t = pl.program_id(1)
    nt = pl.num_programs(1)
    ch = mid_ref.shape[2]
    pad = CONF_KERNEL // 2
    ext_rows = rb + SUBLANES

    def modulated(v):
        v = v.astype(F32)
        y = (v * lax.rsqrt(jnp.mean(v * v, axis=-1, keepdims=True) + EPS)) * g_ref[...]
        return (y * (1.0 + sc_ref[0]) + sh_ref[0]).astype(BF16)

    yb_ref[0:halo, :] = modulated(top_ref[0])
    yb_ref[halo:halo + tm, :] = modulated(mid_ref[0])
    yb_ref[halo + tm:2 * halo + tm, :] = modulated(bot_ref[0])
    top_ok = t > 0
    bot_ok = t < nt - 1

    for ci, c0 in enumerate(range(0, ch, cw)):
        buf = ci % 2
        a = jnp.dot(yb_ref[...], w1_ref[:, c0:c0 + cw], preferred_element_type=F32)
        gg = jnp.dot(yb_ref[...], w1_ref[:, ch + c0:ch + c0 + cw], preferred_element_type=F32)
        glu = (a + b1_ref[:, c0:c0 + cw]) * _sigmoid(gg + b1_ref[:, ch + c0:ch + c0 + cw])
        win_ref[buf, 0:halo, :] = jnp.where(top_ok, glu[0:halo], 0.0)
        win_ref[buf, halo:halo + tm, :] = glu[halo:halo + tm]
        win_ref[buf, halo + tm:2 * halo + tm, :] = jnp.where(bot_ok, glu[halo + tm:], 0.0)
        for l0 in range(0, cw, LANES):
            cs = slice(c0 + l0, c0 + l0 + LANES)
            for r0 in range(0, tm, rb):
                acc = jnp.zeros((rb, LANES), F32) + bdw_ref[:, cs]
                for r in range(SUBLANES):
                    part = None
                    for a8 in range(-2, 2):
                        k = SUBLANES * a8 + r + pad
                        if 0 <= k < CONF_KERNEL:
                            start = halo + r0 + SUBLANES * a8
                            term = (wdw_ref[k:k + 1, cs]
                                    * win_ref[buf, start:start + ext_rows, l0:l0 + LANES])
                            part = term if part is None else part + term
                    acc = acc + _rows_from(part, r, rb)
                cv_ref[r0:r0 + rb, cs] = acc

    hcv = cv_ref[...]
    mu = jnp.mean(hcv, axis=-1, keepdims=True)
    dv = hcv - mu
    yn = dv * lax.rsqrt(jnp.mean(dv * dv, axis=-1, keepdims=True) + EPS)
    yn = _silu(yn * lnw_ref[...] + lnb_ref[...])
    out = jnp.dot(yn.astype(BF16), w2_ref[...], preferred_element_type=F32) + b2_ref[...]
    o_ref[0] = mid_ref[0].astype(F32) + gate_ref[0] * out


def _conformer(hin, g, shift, scale, w1, b1, w_dw, b_dw, ln_w, ln_b, w2, b2, gate, *,
               tm=256, rb=128, halo=16, cw=256, name="conformer"):
    bsz, t, ch = hin.shape
    nh = tm // halo
    last_h = t // halo - 1
    row = lambda b, i: (b, i, 0)
    per_batch = lambda b, i: (b, 0, 0)
    kern = functools.partial(_conf_kernel, tm=tm, rb=rb, halo=halo, cw=cw)
    return pl.pallas_call(
        kern,
        grid=(bsz, t // tm),
        in_specs=[
            pl.BlockSpec((1, halo, ch), lambda b, i: (b, jnp.maximum(i * nh - 1, 0), 0)),
            pl.BlockSpec((1, tm, ch), row),
            pl.BlockSpec((1, halo, ch), lambda b, i: (b, jnp.minimum((i + 1) * nh, last_h), 0)),
            _resident((1, ch)),
            pl.BlockSpec((1, 1, ch), per_batch), pl.BlockSpec((1, 1, ch), per_batch),
            _resident((ch, 2 * ch)), _resident((1, 2 * ch)),
            _resident((CONF_KERNEL, ch)), _resident((1, ch)), _resident((1, ch)),
            _resident((1, ch)), _resident((ch, ch)), _resident((1, ch)),
            pl.BlockSpec((1, 1, ch), per_batch),
        ],
        out_specs=pl.BlockSpec((1, tm, ch), row),
        out_shape=jax.ShapeDtypeStruct((bsz, t, ch), F32),
        scratch_shapes=[pltpu.VMEM((tm + 2 * halo, ch), BF16),
                        pltpu.VMEM((2, tm + 2 * halo, cw), F32), pltpu.VMEM((tm, ch), F32)],
        compiler_params=_params("parallel", "parallel"),
        name=name,
    )(hin, hin, hin, g.reshape(1, ch), shift, scale, w1, b1.reshape(1, 2 * ch), w_dw,
      b_dw.reshape(1, ch), ln_w.reshape(1, ch), ln_b.reshape(1, ch), w2, b2.reshape(1, ch), gate)


def _ffn_kernel(*refs, tm, rb, on_grid, final_norm, hc):
    if final_norm:
        (top_ref, mid_ref, bot_ref, g_ref, sh_ref, sc_ref, wup_ref, cw_ref, cb_ref, wd_ref,
         gate_ref, fn_ref, o_ref, yb_ref, win_ref, val_ref, act_ref) = refs
    else:
        (top_ref, mid_ref, bot_ref, g_ref, sh_ref, sc_ref, wup_ref, cw_ref, cb_ref, wd_ref,
         gate_ref, o_ref, yb_ref, win_ref, val_ref, act_ref) = refs
        fn_ref = None
    halo = GRID_W
    guard = SUBLANES
    t = pl.program_id(1)
    nt = pl.num_programs(1)
    hid = wd_ref.shape[1]
    wup_ref = wup_ref.at[0]
    wd_ref = wd_ref.at[0]

    def modulated(v):
        v = v.astype(F32)
        y = (v * lax.rsqrt(jnp.mean(v * v, axis=-1, keepdims=True) + EPS)) * g_ref[...]
        return (y * (1.0 + sc_ref[0]) + sh_ref[0]).astype(BF16)

    yb_ref[0:halo, :] = modulated(top_ref[0])
    yb_ref[halo:halo + tm, :] = modulated(mid_ref[0])
    yb_ref[halo + tm:2 * halo + tm, :] = modulated(bot_ref[0])
    top_ok = t > 0
    bot_ok = t < nt - 1

    base = guard + halo
    for buf in range(2):
        win_ref[buf, 0:guard, :] = jnp.zeros((guard, hc), F32)
        win_ref[buf, base + tm + halo:base + tm + halo + guard, :] = jnp.zeros((guard, hc), F32)
    col = lax.broadcasted_iota(jnp.int32, (rb, hc), 0) % GRID_W
    row_taps = (-1, 0, 1) if on_grid else (0,)
    ext_rows = rb + 2 * guard
    n_rb = tm // rb
    n_chunks = hid // hc

    def up_piece(j, r):
        buf = j % 2
        c0 = j * hc
        m0 = halo + r * rb
        val_ref[buf, r * rb:(r + 1) * rb, :] = jnp.dot(
            yb_ref[m0:m0 + rb, :], wup_ref[:, c0:c0 + hc], preferred_element_type=F32)
        lo = 0 if r == 0 else m0
        hi = 2 * halo + tm if r == n_rb - 1 else m0 + rb
        gx = jnp.dot(yb_ref[lo:hi, :], wup_ref[:, hid + c0:hid + c0 + hc],
                     preferred_element_type=F32)
        if r == 0:
            win_ref[buf, guard:base, :] = jnp.where(top_ok, gx[0:halo], 0.0)
            gx = gx[halo:]
        if r == n_rb - 1:
            win_ref[buf, base + tm:base + tm + halo, :] = jnp.where(bot_ok, gx[rb:], 0.0)
            gx = gx[:rb]
        win_ref[buf, guard + m0:guard + m0 + rb, :] = gx

    def conv_piece(j, r):
        buf = j % 2
        c0 = j * hc
        parts = []
        for dc in (-1, 0, 1):
            acc = None
            for dr in row_taps:
                start = base + r * rb + dr * GRID_W - guard
                w = cw_ref[(dr + 1) * FFN_CONV + (dc + 1), :, c0:c0 + hc]
                term = w * win_ref[buf, start:start + ext_rows, :]
                acc = term if acc is None else acc + term
            parts.append(_rows_from(acc, guard + dc, rb))
        if on_grid:
            conv = (parts[1] + jnp.where(col >= 1, parts[0], 0.0)
                    + jnp.where(col <= GRID_W - 2, parts[2], 0.0))
        else:
            conv = parts[0] + parts[1] + parts[2]
        val = val_ref[buf, r * rb:(r + 1) * rb, :]
        act_ref[r * rb:(r + 1) * rb, c0:c0 + hc] = (
            _silu(conv + cb_ref[:, c0:c0 + hc]) * val).astype(BF16)

    for r in range(n_rb):
        up_piece(0, r)
    split = (n_chunks // 2) * hc
    down = None
    for j in range(n_chunks):
        for r in range(n_rb):
            conv_piece(j, r)
            if j + 1 < n_chunks:
                up_piece(j + 1, r)
        if (j + 1) * hc == split:
            down = jnp.dot(act_ref[:, 0:split], wd_ref[0:split, :], preferred_element_type=F32)
    down = down + jnp.dot(act_ref[:, split:], wd_ref[split:, :], preferred_element_type=F32)
    hn = mid_ref[0].astype(F32) + gate_ref[0] * down
    if final_norm:
        hn = (hn * lax.rsqrt(jnp.mean(hn * hn, axis=-1, keepdims=True) + EPS)) * fn_ref[...]
    o_ref[0] = hn


def _conv_ffn(hin, g, shift, scale, layer, w_up, conv_w, conv_b, w_down, gate, *, on_grid,
              final_norm_w=None, tm=512, rb=128, hc=256, name="conv_ffn"):
    bsz, t, d = hin.shape
    hid = w_down.shape[1]
    stacked = lambda shape: pl.BlockSpec((1,) + shape, lambda *_: (layer, 0, 0),
                                         pipeline_mode=pl.Buffered(1))
    tm = min(tm, t)
    halo = GRID_W
    nh = tm // halo
    last_h = t // halo - 1
    row = lambda b, i: (b, i, 0)
    per_batch = lambda b, i: (b, 0, 0)
    args = [hin, hin, hin, g.reshape(1, d), shift, scale, w_up,
            conv_w.reshape(FFN_CONV * FFN_CONV, 1, hid), conv_b.reshape(1, hid), w_down, gate]
    specs = [
        pl.BlockSpec((1, halo, d), lambda b, i: (b, jnp.maximum(i * nh - 1, 0), 0)),
        pl.BlockSpec((1, tm, d), row),
        pl.BlockSpec((1, halo, d), lambda b, i: (b, jnp.minimum((i + 1) * nh, last_h), 0)),
        _resident((1, d)),
        pl.BlockSpec((1, 1, d), per_batch), pl.BlockSpec((1, 1, d), per_batch),
        stacked((d, 2 * hid)),
        _resident((FFN_CONV * FFN_CONV, 1, hid)), _resident((1, hid)), stacked((hid, d)),
        pl.BlockSpec((1, 1, d), per_batch),
    ]
    if final_norm_w is not None:
        args.append(final_norm_w.reshape(1, d))
        specs.append(_resident((1, d)))
    kern = functools.partial(_ffn_kernel, tm=tm, on_grid=on_grid,
                             final_norm=final_norm_w is not None, hc=hc, rb=rb)
    return pl.pallas_call(
        kern,
        grid=(bsz, t // tm),
        in_specs=specs,
        out_specs=pl.BlockSpec((1, tm, d), row),
        out_shape=jax.ShapeDtypeStruct((bsz, t, d), F32),
        scratch_shapes=[pltpu.VMEM((tm + 2 * halo, d), BF16),
                        pltpu.VMEM((2, tm + 2 * halo + 2 * SUBLANES, hc), F32),
                        pltpu.VMEM((2, tm, hc), F32), pltpu.VMEM((tm, hid), BF16)],
        compiler_params=_params("parallel", "parallel"),
        name=name,
    )(*args)


def _group_dir_head(v):
    lead = v.shape[:-1]
    v = v.reshape(lead + (2, SSD_GROUPS, SSD_HPG))
    return jnp.swapaxes(v, -3, -2).reshape(lead + (2 * SSD_GROUPS * SSD_HPG,))


def kernel(x, c, ctx, c_ctx, mod_w, mod_b, norm1_w, norm2_w, ssd_w_in, ssd_conv_w, ssd_conv_b, ssd_dt_bias, ssd_a_log, ssd_d, ssd_norm_w, ssd_w_out, conf_w_pw1, conf_b_pw1, conf_w_dw, conf_b_dw, conf_ln_w, conf_ln_b, conf_w_pw2, conf_b_pw2, ffn_w_up, ffn_conv_w, ffn_conv_b, ffn_w_down, final_norm_w):
    bsz, seq, d = x.shape
    depth = mod_w.shape[0]
    g, h, hd, n = SSD_GROUPS, SSD_HPG, SSD_HEAD_DIM, SSD_STATE
    n_heads2 = 2 * g * h

    cond = jnp.concatenate([c, c_ctx[None, :], jnp.zeros((16 - bsz - 1, d), F32)], axis=0)
    mods = _ada_params(cond, mod_w, mod_b).reshape(depth, 16, 6, d)

    def latent_mod(i, k):
        return mods[i, :bsz, k][:, None, :]

    def context_mod(i, k):
        return jnp.broadcast_to(mods[i, bsz, k][None, None, :], (bsz, 1, d))

    w_up, w_down = ffn_w_up.astype(BF16), ffn_w_down.astype(BF16)
    hl, hc = x, ctx
    for i in range(depth):
        kind = i % 2
        j = i // 2
        last = i == depth - 1
        need_ctx = (not last) or kind == 0
        sh1, sc1, g1, sh2, sc2, g2 = (latent_mod(i, k) for k in range(6))
        if need_ctx:
            csh1, csc1, cg1, csh2, csc2, cg2 = (context_mod(i, k) for k in range(6))
        if kind == 0:
            w_in = ssd_w_in[j]
            dt0 = w_in.shape[1] - n_heads2
            w_in = jnp.concatenate(
                [w_in[:, :dt0].astype(BF16), _group_dir_head(w_in[:, dt0:]).astype(BF16),
                 jnp.zeros((d, LANES - n_heads2), BF16)], axis=1)
            lane_pad = lambda v: jnp.pad(_group_dir_head(v.reshape(-1)),
                                         (0, LANES - n_heads2)).reshape(1, LANES)
            dt_bias, a_log = lane_pad(ssd_dt_bias[j]), lane_pad(ssd_a_log[j])
            expand = (jnp.arange(LANES)[:, None] == jnp.arange(n_heads2 * hd)[None, :] // hd
                      ).astype(BF16)
            conv_b = ssd_conv_b[j].reshape(1, -1)
            d_lanes = jnp.repeat(ssd_d[j], hd).reshape(1, g * h * hd)
            w_out = ssd_w_out[j].astype(BF16)

            def mixer(hin, shift, scale, s0, name):
                z, xconv, bt, cmat, dt_raw = _ssd_in_proj(hin, norm1_w[i], shift, scale, w_in,
                                                          ssd_conv_w[j], conv_b, name=name)
                dtf = _dt_factors(dt_raw, dt_bias, a_log, expand)
                return _ssd_scan(z, xconv, bt, cmat, dtf, d_lanes, s0)

            yc, s_ctx = mixer(hc, csh1, csc1, None, "ssd_in_ctx")
            yl, _ = mixer(hl, sh1, sc1, s_ctx, "ssd_in")
            hl = _norm_matmul(yl, ssd_norm_w[j], w_out, post="resid", res=hl, gate=g1,
                              name="ssd_out")
            if not last:
                hc = _norm_matmul(yc, ssd_norm_w[j], w_out, post="resid", res=hc, gate=cg1,
                                  name="ssd_out_ctx")
        else:
            conf = (conf_w_pw1[j].astype(BF16), conf_b_pw1[j], conf_w_dw[j], conf_b_dw[j],
                    conf_ln_w[j], conf_ln_b[j], conf_w_pw2[j].astype(BF16), conf_b_pw2[j])
            hl = _conformer(hl, norm1_w[i], sh1, sc1, *conf, g1)
            if not last:
                hc = _conformer(hc, norm1_w[i], csh1, csc1, *conf, cg1, name="conformer_ctx")
        hl = _conv_ffn(hl, norm2_w[i], sh2, sc2, i, w_up, ffn_conv_w[i], ffn_conv_b[i], w_down, g2,
                       on_grid=True, final_norm_w=final_norm_w if last else None)
        if not last:
            hc = _conv_ffn(hc, norm2_w[i], csh2, csc2, i, w_up, ffn_conv_w[i], ffn_conv_b[i],
                           w_down, cg2, on_grid=False, name="conv_ffn_ctx")
    return hl
```

```python
import functools

import jax
import jax.numpy as jnp
from jax import lax
from jax.experimental import pallas as pl
from jax.experimental.pallas import tpu as pltpu

F32 = jnp.float32
BF16 = jnp.bfloat16

EPS = 1e-6
LANES = 128
SUBLANES = 8
GRID_W = 64
SSD_GROUPS = 8
SSD_HPG = 4
SSD_HEAD_DIM = 64
SSD_STATE = 128
SSD_CHUNK = 128
SSD_CONV = 5
CONF_KERNEL = 31
FFN_CONV = 3

VMEM_LIMIT_BYTES = 56 * 1024 * 1024


def _params(*semantics):
    return pltpu.CompilerParams(dimension_semantics=semantics,
                                vmem_limit_bytes=VMEM_LIMIT_BYTES)


def _sigmoid(v):
    return 0.5 * jnp.tanh(0.5 * v) + 0.5


def _silu(v):
    return v * _sigmoid(v)


def _softplus(v):
    return jnp.maximum(v, 0.0) + jnp.log1p(jnp.exp(-jnp.abs(v)))


def _split3(v):
    v1 = v.astype(BF16)
    r = v - v1.astype(F32)
    v2 = r.astype(BF16)
    r = r - v2.astype(F32)
    return v1, v2, r.astype(BF16)


def _rows_from(ext, start, n):
    total = ext.shape[0]
    if start % SUBLANES == 0:
        return ext[start:start + n]
    lo = (start // SUBLANES) * SUBLANES
    rolled = pltpu.roll(ext, (lo - start) % total, 0)
    return rolled[lo:lo + n]


def _resident(shape):
    nd = len(shape)
    return pl.BlockSpec(shape, lambda *_: (0,) * nd, pipeline_mode=pl.Buffered(1))


def _ada_kernel(cond_ref, w_ref, b_ref, o_ref):
    a = _silu(cond_ref[...]).astype(BF16)
    o_ref[0] = jnp.dot(a, w_ref[0].astype(BF16), preferred_element_type=F32) + b_ref[0]


def _ada_params(cond, mod_w, mod_b, tn=1536):
    depth, d, n = mod_w.shape
    rows = cond.shape[0]
    return pl.pallas_call(
        _ada_kernel,
        grid=(depth, n // tn),
        in_specs=[pl.BlockSpec((rows, d), lambda i, j: (0, 0)),
                  pl.BlockSpec((1, d, tn), lambda i, j: (i, 0, j)),
                  pl.BlockSpec((1, 1, tn), lambda i, j: (i, 0, j))],
        out_specs=pl.BlockSpec((1, rows, tn), lambda i, j: (i, 0, j)),
        out_shape=jax.ShapeDtypeStruct((depth, rows, n), F32),
        compiler_params=_params("parallel", "parallel"),
        name="ada_params",
    )(cond, mod_w, mod_b.reshape(depth, 1, n))


def _mm_kernel(*refs, modulate, post, has_bias, n_out, n_chunk):
    it = iter(refs)
    x_ref = next(it)
    g_ref = next(it)
    sh_ref = sc_ref = b_ref = res_ref = gate_ref = None
    if modulate:
        sh_ref, sc_ref = next(it), next(it)
    w_ref = next(it)
    if has_bias:
        b_ref = next(it)
    if post == "resid":
        res_ref, gate_ref = next(it), next(it)
    o_ref = next(it)

    x = x_ref[0].astype(F32)
    ms = jnp.mean(x * x, axis=-1, keepdims=True)
    y = (x * lax.rsqrt(ms + EPS)) * g_ref[...]
    if modulate:
        y = y * (1.0 + sc_ref[0]) + sh_ref[0]
    yb = y.astype(BF16)

    def proj(c0):
        acc = jnp.dot(yb, w_ref[:, c0:c0 + n_chunk], preferred_element_type=F32)
        if has_bias:
            acc = acc + b_ref[:, c0:c0 + n_chunk]
        return acc

    for n0 in range(0, n_out, n_chunk):
        if post == "glu":
            out = proj(n0) * _sigmoid(proj(n_out + n0))
        elif post == "resid":
            out = res_ref[0, :, n0:n0 + n_chunk] + gate_ref[0, :, n0:n0 + n_chunk] * proj(n0)
        else:
            out = proj(n0)
        o_ref[0, :, n0:n0 + n_chunk] = out.astype(o_ref.dtype)


def _norm_matmul(x, g, w, *, shift=None, scale=None, bias=None, post="store",
                 res=None, gate=None, out_dtype=F32, tm=512, n_chunk=1024, name=None):
    bsz, t, k = x.shape
    nw = w.shape[1]
    n_out = nw // 2 if post == "glu" else nw
    tm = min(tm, t)
    n_chunk = max(c for c in range(LANES, n_chunk + 1, LANES) if n_out % c == 0)
    assert t % tm == 0
    modulate = shift is not None
    row = lambda b, i: (b, i, 0)
    per_batch = lambda b, i: (b, 0, 0)
    args = [x, g.reshape(1, k)]
    specs = [pl.BlockSpec((1, tm, k), row), _resident((1, k))]
    if modulate:
        args += [shift, scale]
        specs += [pl.BlockSpec((1, 1, k), per_batch)] * 2
    args.append(w)
    specs.append(_resident((k, nw)))
    if bias is not None:
        args.append(bias.reshape(1, nw))
        specs.append(_resident((1, nw)))
    if post == "resid":
        args += [res, gate]
        specs += [pl.BlockSpec((1, tm, n_out), row), pl.BlockSpec((1, 1, n_out), per_batch)]
    kern = functools.partial(_mm_kernel, modulate=modulate, post=post,
                             has_bias=bias is not None, n_out=n_out, n_chunk=n_chunk)
    return pl.pallas_call(
        kern,
        grid=(bsz, t // tm),
        in_specs=specs,
        out_specs=pl.BlockSpec((1, tm, n_out), row),
        out_shape=jax.ShapeDtypeStruct((bsz, t, n_out), out_dtype),
        compiler_params=_params("parallel", "parallel"),
        name=name,
    )(*args)


def _dt_kernel(raw_ref, bias_ref, alog_ref, e_ref,
               cumc_ref, wc_ref, eac_ref, dtr_ref, cumr_ref, dec_ref, tot_ref, *, seq):
    q = SSD_CHUNK
    nc = seq // q
    ii = lax.broadcasted_iota(jnp.int32, (q, q), 0)
    jj = lax.broadcasted_iota(jnp.int32, (q, q), 1)
    lower_b = jnp.where(jj <= ii, 1.0, 0.0).astype(BF16)
    upper_b = jnp.where(jj >= ii, 1.0, 0.0).astype(BF16)
    lane = lax.broadcasted_iota(jnp.int32, (1, LANES), 1)
    backward = (lane // SSD_HPG) % 2 == 1
    bias = bias_ref[...]
    a_neg = -jnp.exp(alog_ref[...])

    def body(ci, carry):
        rows = pl.ds(pl.multiple_of(ci * q, q), q)
        dt = _softplus(raw_ref[0, rows, :].astype(F32) + bias)
        parts = _split3(dt * a_neg)
        cum_f = sum(jnp.dot(lower_b, p, preferred_element_type=F32) for p in parts)
        cum_b = sum(jnp.dot(upper_b, p, preferred_element_type=F32) for p in parts)
        cum = jnp.where(backward, cum_b, cum_f)
        tot = cum_f[q - 1:q, :]
        cumc_ref[0, rows, :] = cum
        wc_ref[0, rows, :] = (dt * jnp.exp(tot - cum)).astype(BF16)
        eac_ref[0, rows, :] = jnp.exp(cum).astype(BF16)
        dtr_ref[0, :, rows] = dt.T
        cumr_ref[0, :, rows] = cum.T
        tot_ref[pl.ds(ci, 1), :] = tot
        return carry

    lax.fori_loop(0, nc, body, 0, unroll=min(4, nc))
    dec = jnp.exp(tot_ref[...])
    dec_ref[0] = sum(jnp.dot(p, e_ref[...], preferred_element_type=F32) for p in _split3(dec))


def _dt_factors(proj, bias, alog, expand):
    bsz, seq, width = proj.shape
    nc = seq // SSD_CHUNK
    blk = width // LANES - 1
    n_exp = expand.shape[1]
    col = pl.BlockSpec((1, seq, LANES), lambda b: (b, 0, 0))
    rowb = pl.BlockSpec((1, LANES, seq), lambda b: (b, 0, 0))
    return pl.pallas_call(
        functools.partial(_dt_kernel, seq=seq),
        grid=(bsz,),
        in_specs=[pl.BlockSpec((1, seq, LANES), lambda b: (b, 0, blk)),
                  _resident((1, LANES)), _resident((1, LANES)), _resident((LANES, n_exp))],
        out_specs=[col, col, col, rowb, rowb,
                   pl.BlockSpec((1, nc, n_exp), lambda b: (b, 0, 0))],
        out_shape=[jax.ShapeDtypeStruct((bsz, seq, LANES), F32),
                   jax.ShapeDtypeStruct((bsz, seq, LANES), BF16),
                   jax.ShapeDtypeStruct((bsz, seq, LANES), BF16),
                   jax.ShapeDtypeStruct((bsz, LANES, seq), F32),
                   jax.ShapeDtypeStruct((bsz, LANES, seq), F32),
                   jax.ShapeDtypeStruct((bsz, nc, n_exp), F32)],
        scratch_shapes=[pltpu.VMEM((nc, LANES), F32)],
        compiler_params=_params("parallel"),
        name="ssd_dt",
    )(proj, bias, alog, expand)


def _ssd_kernel(*refs, seq, zero_init):
    if zero_init:
        (x_ref, bt_ref, c_ref, z_ref, cumc_ref, wc_ref, eac_ref, dtr_ref, cumr_ref, dec_ref,
         d_ref, y_ref, sn_ref, sc_ref, sent_ref, ccol_ref, st_ref, e_ref) = refs
        s0_ref = None
    else:
        (x_ref, bt_ref, c_ref, z_ref, cumc_ref, wc_ref, eac_ref, dtr_ref, cumr_ref, dec_ref,
         d_ref, s0_ref, y_ref, sn_ref, sc_ref, sent_ref, ccol_ref, st_ref, e_ref) = refs
    q = SSD_CHUNK
    nc = seq // q
    nh, hd = SSD_HPG, SSD_HEAD_DIM
    xw = nh * hd
    g = pl.program_id(1)

    kk = lax.broadcasted_iota(jnp.int32, (LANES, xw), 0)
    head_of_lane = lax.broadcasted_iota(jnp.int32, (LANES, xw), 1) // hd
    for d in range(2):
        e_ref[d] = jnp.where(kk == g * 2 * nh + d * nh + head_of_lane, 1.0, 0.0).astype(BF16)

    for gg in range(SSD_GROUPS):
        @pl.when(g == gg)
        def _():
            def cp_body(ci, carry):
                rows = pl.ds(pl.multiple_of(ci * q, q), q)
                ccol_ref[rows, :] = cumc_ref[0, rows, gg * 2 * nh:(gg + 1) * 2 * nh]
                return carry
            lax.fori_loop(0, nc, cp_body, 0, unroll=True)

    ii = lax.broadcasted_iota(jnp.int32, (q, q), 0)
    jj = lax.broadcasted_iota(jnp.int32, (q, q), 1)
    lane_head = lax.broadcasted_iota(jnp.int32, (q, xw), 1) // hd

    def intra(ci, carry):
        rows = pl.ds(pl.multiple_of(ci * q, q), q)
        xc = x_ref[0, rows, :]
        x_b = xc.astype(BF16)
        btb = bt_ref[0, 0, :, rows]
        cbf = c_ref[0, rows, :]
        cb = jnp.dot(cbf, btb, preferred_element_type=F32)
        cum8 = ccol_ref[rows, :]
        lm = []
        for h in range(nh):
            acf, acb = cum8[:, h:h + 1], cum8[:, nh + h:nh + h + 1]
            arf = cumr_ref[0, h:h + 1, rows]
            arb = cumr_ref[0, nh + h:nh + h + 1, rows]
            dtf = dtr_ref[0, h:h + 1, rows]
            dtb = dtr_ref[0, nh + h:nh + h + 1, rows]
            arg = jnp.where(jj <= ii, acf - arf, acb - arb)
            dts = jnp.where(jj < ii, dtf, jnp.where(jj > ii, dtb, dtf + dtb))
            lm.append((jnp.exp(arg) * (cb * dts)).astype(BF16))
        masked = [jnp.where(lane_head == h, x_b, jnp.zeros_like(x_b)) for h in range(nh)]
        yd = (jnp.dot(jnp.concatenate(lm[0:2], axis=1), jnp.concatenate(masked[0:2], axis=0),
                      preferred_element_type=F32)
              + jnp.dot(jnp.concatenate(lm[2:4], axis=1), jnp.concatenate(masked[2:4], axis=0),
                        preferred_element_type=F32))
        y_ref[0, rows, :] = d_ref[...] * xc + yd
        for d in range(2):
            wexp = jnp.dot(wc_ref[0, rows, :], e_ref[d], preferred_element_type=F32)
            sc_ref[ci, d] = jnp.dot(btb, (xc * wexp).astype(BF16), preferred_element_type=F32)
        return carry

    lax.fori_loop(0, nc, intra, 0, unroll=min(16, nc))

    st_ref[...] = jnp.zeros(st_ref.shape, F32) if zero_init else s0_ref[0, 0]

    def carry_states(c, carry):
        for d in range(2):
            ci = c if d == 0 else nc - 1 - c
            s_in = st_ref[d]
            sent_ref[ci, d] = s_in.astype(BF16)
            dec = dec_ref[0, pl.ds(ci, 1), d * xw:(d + 1) * xw]
            st_ref[d] = dec * s_in + sc_ref[ci, d]
        return carry

    lax.fori_loop(0, nc, carry_states, 0, unroll=2)
    sn_ref[0, 0] = st_ref[...]

    def inter(ci, carry):
        rows = pl.ds(pl.multiple_of(ci * q, q), q)
        cbf = c_ref[0, rows, :]
        y = y_ref[0, rows, :]
        for d in range(2):
            ea = jnp.dot(eac_ref[0, rows, :], e_ref[d], preferred_element_type=F32)
            y = y + ea * jnp.dot(cbf, sent_ref[ci, d], preferred_element_type=F32)
        y_ref[0, rows, :] = (y * _silu(z_ref[0, rows, :].astype(F32))).astype(y_ref.dtype)
        return carry

    lax.fori_loop(0, nc, inter, 0, unroll=min(8, nc))


def _ssd_scan(z, xconv, bt, cmat, dtf, d_lanes, s0):
    cumc, wc, eac, dtr, cumr, dec = dtf
    bsz, seq, di = z.shape
    g, h, hd, n = SSD_GROUPS, SSD_HPG, SSD_HEAD_DIM, SSD_STATE
    q = SSD_CHUNK
    nc = seq // q
    xw = h * hd
    state_spec = pl.BlockSpec((1, 1, 2, n, xw), lambda b, i: (b, i, 0, 0, 0))
    chan = pl.BlockSpec((1, seq, xw), lambda b, i: (b, 0, i))
    col = pl.BlockSpec((1, seq, LANES), lambda b, i: (b, 0, 0))
    rowb = pl.BlockSpec((1, 2 * h, seq), lambda b, i: (b, i, 0))
    kern = functools.partial(_ssd_kernel, seq=seq, zero_init=s0 is None)
    args = [xconv, bt, cmat, z, cumc, wc, eac, dtr, cumr, dec, d_lanes]
    specs = [
        chan,
        pl.BlockSpec((1, 1, n, seq), lambda b, i: (b, i, 0, 0)),
        pl.BlockSpec((1, seq, n), lambda b, i: (b, 0, i)),
        chan,
        col, col, col, rowb, rowb,
        pl.BlockSpec((1, nc, 2 * xw), lambda b, i: (b, 0, i)),
        pl.BlockSpec((1, xw), lambda b, i: (0, i)),
    ]
    if s0 is not None:
        args.append(s0)
        specs.append(state_spec)
    return pl.pallas_call(
        kern,
        grid=(bsz, g),
        in_specs=specs,
        out_specs=[chan, state_spec],
        out_shape=[jax.ShapeDtypeStruct((bsz, seq, di), F32),
                   jax.ShapeDtypeStruct((bsz, g, 2, n, xw), F32)],
        scratch_shapes=[pltpu.VMEM((nc, 2, n, xw), F32),
                        pltpu.VMEM((nc, 2, n, xw), BF16), pltpu.VMEM((seq, 2 * h), F32),
                        pltpu.VMEM((2, n, xw), F32), pltpu.VMEM((2, LANES, xw), BF16)],
        compiler_params=_params("parallel", "parallel"),
        name="ssd_scan",
    )(*args)


def _ssd_in_kernel(top_ref, mid_ref, bot_ref, g_ref, sh_ref, sc_ref, w_ref, cw_ref, cb_ref,
                   z_ref, x_ref, bt_ref, c_ref, dt_ref, yb_ref, win_ref, *, tm, rb, cw, nz):
    halo = SUBLANES
    t = pl.program_id(1)
    nt = pl.num_programs(1)
    di = z_ref.shape[2]
    n_state = c_ref.shape[2]

    def modulated(v):
        v = v.astype(F32)
        y = (v * lax.rsqrt(jnp.mean(v * v, axis=-1, keepdims=True) + EPS)) * g_ref[...]
        return (y * (1.0 + sc_ref[0]) + sh_ref[0]).astype(BF16)

    yb_ref[0:halo, :] = modulated(top_ref[0])
    yb_ref[halo:halo + tm, :] = modulated(mid_ref[0])
    yb_ref[halo + tm:2 * halo + tm, :] = modulated(bot_ref[0])
    top_ok = t > 0
    bot_ok = t < nt - 1

    dt0 = w_ref.shape[1] - LANES
    dt_ref[0] = jnp.dot(yb_ref[halo:halo + tm, :], w_ref[:, dt0:], preferred_element_type=F32)

    n_conv = 2 * di
    z_every = (n_conv // cw) // (di // nz)
    for ci, c0 in enumerate(range(0, n_conv, cw)):
        buf = ci % 2
        if ci % z_every == 0:
            n0 = (ci // z_every) * nz
            z_ref[0, :, n0:n0 + nz] = jnp.dot(yb_ref[halo:halo + tm, :], w_ref[:, n0:n0 + nz],
                                              preferred_element_type=F32).astype(z_ref.dtype)
        p = jnp.dot(yb_ref[...], w_ref[:, di + c0:di + c0 + cw], preferred_element_type=F32)
        win_ref[buf, 0:halo, :] = jnp.where(top_ok, p[0:halo], 0.0)
        win_ref[buf, halo:halo + tm, :] = p[halo:halo + tm]
        win_ref[buf, halo + tm:2 * halo + tm, :] = jnp.where(bot_ok, p[halo + tm:], 0.0)
        w = cw_ref[:, c0:c0 + cw]
        bias = cb_ref[:, c0:c0 + cw]
        for r0 in range(0, tm, rb):
            ext = win_ref[buf, r0:r0 + rb + 2 * halo, :]
            acc = bias + w[SSD_CONV // 2:SSD_CONV // 2 + 1, :] * ext[halo:halo + rb]
            for k in range(SSD_CONV):
                if k != SSD_CONV // 2:
                    acc = acc + w[k:k + 1, :] * _rows_from(ext, halo - SSD_CONV // 2 + k, rb)
            out = _silu(acc)
            if c0 < di:
                x_ref[0, r0:r0 + rb, c0:c0 + cw] = out
            elif c0 < di + n_state:
                for l0 in range(0, cw, SSD_STATE):
                    grp = (c0 - di + l0) // SSD_STATE
                    bt_ref[0, grp, :, r0:r0 + rb] = out[:, l0:l0 + SSD_STATE].T.astype(bt_ref.dtype)
            else:
                c1 = c0 - di - n_state
                c_ref[0, r0:r0 + rb, c1:c1 + cw] = out.astype(c_ref.dtype)


def _ssd_in_proj(hin, g, shift, scale, w_in, conv_w, conv_b, *, tm=512, rb=128, cw=256, nz=512,
                 name="ssd_in"):
    bsz, t, d = hin.shape
    nw = w_in.shape[1]
    di = (nw - LANES) // 3
    tm = min(tm, t)
    halo = SUBLANES
    nh = tm // halo
    last_h = t // halo - 1
    row = lambda b, i: (b, i, 0)
    per_batch = lambda b, i: (b, 0, 0)
    kern = functools.partial(_ssd_in_kernel, tm=tm, rb=rb, cw=cw, nz=nz)
    return pl.pallas_call(
        kern,
        grid=(bsz, t // tm),
        in_specs=[
            pl.BlockSpec((1, halo, d), lambda b, i: (b, jnp.maximum(i * nh - 1, 0), 0)),
            pl.BlockSpec((1, tm, d), row),
            pl.BlockSpec((1, halo, d), lambda b, i: (b, jnp.minimum((i + 1) * nh, last_h), 0)),
            _resident((1, d)),
            pl.BlockSpec((1, 1, d), per_batch), pl.BlockSpec((1, 1, d), per_batch),
            _resident((d, nw)), _resident((SSD_CONV, 2 * di)), _resident((1, 2 * di)),
        ],
        out_specs=[pl.BlockSpec((1, tm, di), row), pl.BlockSpec((1, tm, di), row),
                   pl.BlockSpec((1, SSD_GROUPS, SSD_STATE, tm), lambda b, i: (b, 0, 0, i)),
                   pl.BlockSpec((1, tm, di // 2), row), pl.BlockSpec((1, tm, LANES), row)],
        out_shape=[jax.ShapeDtypeStruct((bsz, t, di), F32),
                   jax.ShapeDtypeStruct((bsz, t, di), F32),
                   jax.ShapeDtypeStruct((bsz, SSD_GROUPS, SSD_STATE, t), BF16),
                   jax.ShapeDtypeStruct((bsz, t, di // 2), BF16),
                   jax.ShapeDtypeStruct((bsz, t, LANES), F32)],
        scratch_shapes=[pltpu.VMEM((tm + 2 * halo, d), BF16),
                        pltpu.VMEM((2, tm + 2 * halo, cw), F32)],
        compiler_params=_params("parallel", "parallel"),
        name=name,
    )(hin, hin, hin, g.reshape(1, d), shift, scale, w_in, conv_w, conv_b)


def _conf_kernel(top_ref, mid_ref, bot_ref, g_ref, sh_ref, sc_ref, w1_ref, b1_ref, wdw_ref,
                 bdw_ref, lnw_ref, lnb_ref, w2_ref, b2_ref, gate_ref, o_ref,
                 yb_ref, win_ref, cv_ref, *, tm, rb, halo, cw):
    t = pl.program_id(1)
    nt = pl.num_programs(1)
    ch = mid_ref.shape[2]
    pad = CONF_KERNEL // 2
    ext_rows = rb + SUBLANES

    def modulated(v):
        v = v.astype(F32)
        y = (v * lax.rsqrt(jnp.mean(v * v, axis=-1, keepdims=True) + EPS)) * g_ref[...]
        return (y * (1.0 + sc_ref[0]) + sh_ref[0]).astype(BF16)

    yb_ref[0:halo, :] = modulated(top_ref[0])
    yb_ref[halo:halo + tm, :] = modulated(mid_ref[0])
    yb_ref[halo + tm:2 * halo + tm, :] = modulated(bot_ref[0])
    top_ok = t > 0
    bot_ok = t < nt - 1

    for ci, c0 in enumerate(range(0, ch, cw)):
        buf = ci % 2
        a = jnp.dot(yb_ref[...], w1_ref[:, c0:c0 + cw], preferred_element_type=F32)
        gg = jnp.dot(yb_ref[...], w1_ref[:, ch + c0:ch + c0 + cw], preferred_element_type=F32)
        glu = (a + b1_ref[:, c0:c0 + cw]) * _sigmoid(gg + b1_ref[:, ch + c0:ch + c0 + cw])
        win_ref[buf, 0:halo, :] = jnp.where(top_ok, glu[0:halo], 0.0)
        win_ref[buf, halo:halo + tm, :] = glu[halo:halo + tm]
        win_ref[buf, halo + tm:2 * halo + tm, :] = jnp.where(bot_ok, glu[halo + tm:], 0.0)
        for l0 in range(0, cw, LANES):
            cs = slice(c0 + l0, c0 + l0 + LANES)
            for r0 in range(0, tm, rb):
                acc = jnp.zeros((rb, LANES), F32) + bdw_ref[:, cs]
                for r in range(SUBLANES):
                    part = None
                    for a8 in range(-2, 2):
                        k = SUBLANES * a8 + r + pad
                        if 0 <= k < CONF_KERNEL:
                            start = halo + r0 + SUBLANES * a8
                            term = (wdw_ref[k:k + 1, cs]
                                    * win_ref[buf, start:start + ext_rows, l0:l0 + LANES])
                            part = term if part is None else part + term
                    acc = acc + _rows_from(part, r, rb)
                cv_ref[r0:r0 + rb, cs] = acc

    hcv = cv_ref[...]
    mu = jnp.mean(hcv, axis=-1, keepdims=True)
    dv = hcv - mu
    yn = dv * lax.rsqrt(jnp.mean(dv * dv, axis=-1, keepdims=True) + EPS)
    yn = _silu(yn * lnw_ref[...] + lnb_ref[...])
    out = jnp.dot(yn.astype(BF16), w2_ref[...], preferred_element_type=F32) + b2_ref[...]
    o_ref[0] = mid_ref[0].astype(F32) + gate_ref[0] * out


def _conformer(hin, g, shift, scale, w1, b1, w_dw, b_dw, ln_w, ln_b, w2, b2, gate, *,
               tm=256, rb=128, halo=16, cw=256, name="conformer"):
    bsz, t, ch = hin.shape
    nh = tm // halo
    last_h = t // halo - 1
    row = lambda b, i: (b, i, 0)
    per_batch = lambda b, i: (b, 0, 0)
    kern = functools.partial(_conf_kernel, tm=tm, rb=rb, halo=halo, cw=cw)
    return pl.pallas_call(
        kern,
        grid=(bsz, t // tm),
        in_specs=[
            pl.BlockSpec((1, halo, ch), lambda b, i: (b, jnp.maximum(i * nh - 1, 0), 0)),
            pl.BlockSpec((1, tm, ch), row),
            pl.BlockSpec((1, halo, ch), lambda b, i: (b, jnp.minimum((i + 1) * nh, last_h), 0)),
            _resident((1, ch)),
            pl.BlockSpec((1, 1, ch), per_batch), pl.BlockSpec((1, 1, ch), per_batch),
            _resident((ch, 2 * ch)), _resident((1, 2 * ch)),
            _resident((CONF_KERNEL, ch)), _resident((1, ch)), _resident((1, ch)),
            _resident((1, ch)), _resident((ch, ch)), _resident((1, ch)),
            pl.BlockSpec((1, 1, ch), per_batch),
        ],
        out_specs=pl.BlockSpec((1, tm, ch), row),
        out_shape=jax.ShapeDtypeStruct((bsz, t, ch), F32),
        scratch_shapes=[pltpu.VMEM((tm + 2 * halo, ch), BF16),
                        pltpu.VMEM((2, tm + 2 * halo, cw), F32), pltpu.VMEM((tm, ch), F32)],
        compiler_params=_params("parallel", "parallel"),
        name=name,
    )(hin, hin, hin, g.reshape(1, ch), shift, scale, w1, b1.reshape(1, 2 * ch), w_dw,
      b_dw.reshape(1, ch), ln_w.reshape(1, ch), ln_b.reshape(1, ch), w2, b2.reshape(1, ch), gate)


def _ffn_kernel(*refs, tm, rb, on_grid, final_norm, hc):
    if final_norm:
        (top_ref, mid_ref, bot_ref, g_ref, sh_ref, sc_ref, wup_ref, cw_ref, cb_ref, wd_ref,
         gate_ref, fn_ref, o_ref, yb_ref, win_ref, val_ref, act_ref) = refs
    else:
        (top_ref, mid_ref, bot_ref, g_ref, sh_ref, sc_ref, wup_ref, cw_ref, cb_ref, wd_ref,
         gate_ref, o_ref, yb_ref, win_ref, val_ref, act_ref) = refs
        fn_ref = None
    halo = GRID_W
    guard = SUBLANES
    t = pl.program_id(1)
    nt = pl.num_programs(1)
    hid = wd_ref.shape[1]
    wup_ref = wup_ref.at[0]
    wd_ref = wd_ref.at[0]

    def modulated(v):
        v = v.astype(F32)
        y = (v * lax.rsqrt(jnp.mean(v * v, axis=-1, keepdims=True) + EPS)) * g_ref[...]
        return (y * (1.0 + sc_ref[0]) + sh_ref[0]).astype(BF16)

    yb_ref[0:halo, :] = modulated(top_ref[0])
    yb_ref[halo:halo + tm, :] = modulated(mid_ref[0])
    yb_ref[halo + tm:2 * halo + tm, :] = modulated(bot_ref[0])
    top_ok = t > 0
    bot_ok = t < nt - 1

    base = guard + halo
    for buf in range(2):
        win_ref[buf, 0:guard, :] = jnp.zeros((guard, hc), F32)
        win_ref[buf, base + tm + halo:base + tm + halo + guard, :] = jnp.zeros((guard, hc), F32)
    col = lax.broadcasted_iota(jnp.int32, (rb, hc), 0) % GRID_W
    row_taps = (-1, 0, 1) if on_grid else (0,)
    ext_rows = rb + 2 * guard
    n_rb = tm // rb
    n_chunks = hid // hc

    def up_piece(j, r):
        buf = j % 2
        c0 = j * hc
        m0 = halo + r * rb
        val_ref[buf, r * rb:(r + 1) * rb, :] = jnp.dot(
            yb_ref[m0:m0 + rb, :], wup_ref[:, c0:c0 + hc], preferred_element_type=F32)
        lo = 0 if r == 0 else m0
        hi = 2 * halo + tm if r == n_rb - 1 else m0 + rb
        gx = jnp.dot(yb_ref[lo:hi, :], wup_ref[:, hid + c0:hid + c0 + hc],
                     preferred_element_type=F32)
        if r == 0:
            win_ref[buf, guard:base, :] = jnp.where(top_ok, gx[0:halo], 0.0)
            gx = gx[halo:]
        if r == n_rb - 1:
            win_ref[buf, base + tm:base + tm + halo, :] = jnp.where(bot_ok, gx[rb:], 0.0)
            gx = gx[:rb]
        win_ref[buf, guard + m0:guard + m0 + rb, :] = gx

    def conv_piece(j, r):
        buf = j % 2
        c0 = j * hc
        parts = []
        for dc in (-1, 0, 1):
            acc = None
            for dr in row_taps:
                start = base + r * rb + dr * GRID_W - guard
                w = cw_ref[(dr + 1) * FFN_CONV + (dc + 1), :, c0:c0 + hc]
                term = w * win_ref[buf, start:start + ext_rows, :]
                acc = term if acc is None else acc + term
            parts.append(_rows_from(acc, guard + dc, rb))
        if on_grid:
            conv = (parts[1] + jnp.where(col >= 1, parts[0], 0.0)
                    + jnp.where(col <= GRID_W - 2, parts[2], 0.0))
        else:
            conv = parts[0] + parts[1] + parts[2]
        val = val_ref[buf, r * rb:(r + 1) * rb, :]
        act_ref[r * rb:(r + 1) * rb, c0:c0 + hc] = (
            _silu(conv + cb_ref[:, c0:c0 + hc]) * val).astype(BF16)

    for r in range(n_rb):
        up_piece(0, r)
    split = (n_chunks // 2) * hc
    down = None
    for j in range(n_chunks):
        for r in range(n_rb):
            conv_piece(j, r)
            if j + 1 < n_chunks:
                up_piece(j + 1, r)
        if (j + 1) * hc == split:
            down = jnp.dot(act_ref[:, 0:split], wd_ref[0:split, :], preferred_element_type=F32)
    down = down + jnp.dot(act_ref[:, split:], wd_ref[split:, :], preferred_element_type=F32)
    hn = mid_ref[0].astype(F32) + gate_ref[0] * down
    if final_norm:
        hn = (hn * lax.rsqrt(jnp.mean(hn * hn, axis=-1, keepdims=True) + EPS)) * fn_ref[...]
    o_ref[0] = hn


def _conv_ffn(hin, g, shift, scale, layer, w_up, conv_w, conv_b, w_down, gate, *, on_grid,
              final_norm_w=None, tm=512, rb=128, hc=256, name="conv_ffn"):
    bsz, t, d = hin.shape
    hid = w_down.shape[1]
    stacked = lambda shape: pl.BlockSpec((1,) + shape, lambda *_: (layer, 0, 0),
                                         pipeline_mode=pl.Buffered(1))
    tm = min(tm, t)
    halo = GRID_W
    nh = tm // halo
    last_h = t // halo - 1
    row = lambda b, i: (b, i, 0)
    per_batch = lambda b, i: (b, 0, 0)
    args = [hin, hin, hin, g.reshape(1, d), shift, scale, w_up,
            conv_w.reshape(FFN_CONV * FFN_CONV, 1, hid), conv_b.reshape(1, hid), w_down, gate]
    specs = [
        pl.BlockSpec((1, halo, d), lambda b, i: (b, jnp.maximum(i * nh - 1, 0), 0)),
        pl.BlockSpec((1, tm, d), row),
        pl.BlockSpec((1, halo, d), lambda b, i: (b, jnp.minimum((i + 1) * nh, last_h), 0)),
        _resident((1, d)),
        pl.BlockSpec((1, 1, d), per_batch), pl.BlockSpec((1, 1, d), per_batch),
        stacked((d, 2 * hid)),
        _resident((FFN_CONV * FFN_CONV, 1, hid)), _resident((1, hid)), stacked((hid, d)),
        pl.BlockSpec((1, 1, d), per_batch),
    ]
    if final_norm_w is not None:
        args.append(final_norm_w.reshape(1, d))
        specs.append(_resident((1, d)))
    kern = functools.partial(_ffn_kernel, tm=tm, on_grid=on_grid,
                             final_norm=final_norm_w is not None, hc=hc, rb=rb)
    return pl.pallas_call(
        kern,
        grid=(bsz, t // tm),
        in_specs=specs,
        out_specs=pl.BlockSpec((1, tm, d), row),
        out_shape=jax.ShapeDtypeStruct((bsz, t, d), F32),
        scratch_shapes=[pltpu.VMEM((tm + 2 * halo, d), BF16),
                        pltpu.VMEM((2, tm + 2 * halo + 2 * SUBLANES, hc), F32),
                        pltpu.VMEM((2, tm, hc), F32), pltpu.VMEM((tm, hid), BF16)],
        compiler_params=_params("parallel", "parallel"),
        name=name,
    )(*args)


def _group_dir_head(v):
    lead = v.shape[:-1]
    v = v.reshape(lead + (2, SSD_GROUPS, SSD_HPG))
    return jnp.swapaxes(v, -3, -2).reshape(lead + (2 * SSD_GROUPS * SSD_HPG,))


def kernel(x, c, ctx, c_ctx, mod_w, mod_b, norm1_w, norm2_w, ssd_w_in, ssd_conv_w, ssd_conv_b, ssd_dt_bias, ssd_a_log, ssd_d, ssd_norm_w, ssd_w_out, conf_w_pw1, conf_b_pw1, conf_w_dw, conf_b_dw, conf_ln_w, conf_ln_b, conf_w_pw2, conf_b_pw2, ffn_w_up, ffn_conv_w, ffn_conv_b, ffn_w_down, final_norm_w):
    bsz, seq, d = x.shape
    depth = mod_w.shape[0]
    g, h, hd, n = SSD_GROUPS, SSD_HPG, SSD_HEAD_DIM, SSD_STATE
    n_heads2 = 2 * g * h

    cond = jnp.concatenate([c, c_ctx[None, :], jnp.zeros((16 - bsz - 1, d), F32)], axis=0)
    mods = _ada_params(cond, mod_w, mod_b).reshape(depth, 16, 6, d)

    def latent_mod(i, k):
        return mods[i, :bsz, k][:, None, :]

    def context_mod(i, k):
        return jnp.broadcast_to(mods[i, bsz, k][None, None, :], (bsz, 1, d))

    w_up, w_down = ffn_w_up.astype(BF16), ffn_w_down.astype(BF16)
    hl, hc = x, ctx
    for i in range(depth):
        kind = i % 2
        j = i // 2
        last = i == depth - 1
        need_ctx = (not last) or kind == 0
        sh1, sc1, g1, sh2, sc2, g2 = (latent_mod(i, k) for k in range(6))
        if need_ctx:
            csh1, csc1, cg1, csh2, csc2, cg2 = (context_mod(i, k) for k in range(6))
        if kind == 0:
            w_in = ssd_w_in[j]
            dt0 = w_in.shape[1] - n_heads2
            w_in = jnp.concatenate(
                [w_in[:, :dt0].astype(BF16), _group_dir_head(w_in[:, dt0:]).astype(BF16),
                 jnp.zeros((d, LANES - n_heads2), BF16)], axis=1)
            lane_pad = lambda v: jnp.pad(_group_dir_head(v.reshape(-1)),
                                         (0, LANES - n_heads2)).reshape(1, LANES)
            dt_bias, a_log = lane_pad(ssd_dt_bias[j]), lane_pad(ssd_a_log[j])
            expand = (jnp.arange(LANES)[:, None] == jnp.arange(n_heads2 * hd)[None, :] // hd
                      ).astype(BF16)
            conv_b = ssd_conv_b[j].reshape(1, -1)
            d_lanes = jnp.repeat(ssd_d[j], hd).reshape(1, g * h * hd)
            w_out = ssd_w_out[j].astype(BF16)

            def mixer(hin, shift, scale, s0, name):
                z, xconv, bt, cmat, dt_raw = _ssd_in_proj(hin, norm1_w[i], shift, scale, w_in,
                                                          ssd_conv_w[j], conv_b, name=name)
                dtf = _dt_factors(dt_raw, dt_bias, a_log, expand)
                return _ssd_scan(z, xconv, bt, cmat, dtf, d_lanes, s0)

            yc, s_ctx = mixer(hc, csh1, csc1, None, "ssd_in_ctx")
            yl, _ = mixer(hl, sh1, sc1, s_ctx, "ssd_in")
            hl = _norm_matmul(yl, ssd_norm_w[j], w_out, post="resid", res=hl, gate=g1,
                              name="ssd_out")
            if not last:
                hc = _norm_matmul(yc, ssd_norm_w[j], w_out, post="resid", res=hc, gate=cg1,
                                  name="ssd_out_ctx")
        else:
            conf = (conf_w_pw1[j].astype(BF16), conf_b_pw1[j], conf_w_dw[j], conf_b_dw[j],
                    conf_ln_w[j], conf_ln_b[j], conf_w_pw2[j].astype(BF16), conf_b_pw2[j])
            hl = _conformer(hl, norm1_w[i], sh1, sc1, *conf, g1)
            if not last:
                hc = _conformer(hc, norm1_w[i], csh1, csc1, *conf, cg1, name="conformer_ctx")
        hl = _conv_ffn(hl, norm2_w[i], sh2, sc2, i, w_up, ffn_conv_w[i], ffn_conv_b[i], w_down, g2,
                       on_grid=True, final_norm_w=final_norm_w if last else None)
        if not last:
            hc = _conv_ffn(hc, norm2_w[i], csh2, csc2, i, w_up, ffn_conv_w[i], ffn_conv_b[i],
                           w_down, cg2, on_grid=False, name="conv_ffn_ctx")
    return hl
```
